```python
import math
import jax, jax.numpy as jnp
from jax import lax
import numpy as np

D_MODEL = 1024
BATCH = 32
SEQ = 2048
DEPTH = 4

GRID_W = 64
ROPE_THETA = 10000.0
Q_BLOCK = 128
CHUNK = 128
LN_EPS = 1e-5

A_HEADS = 4
A_QK_DIM = 32
A_V_DIM = 64
A_WIDTH = A_HEADS * A_V_DIM
B_HEADS = 8
B_KV_HEADS = 2
B_HEAD_DIM = 64
B_GROUP = B_HEADS // B_KV_HEADS
B_WIDTH = B_HEADS * B_HEAD_DIM
C_GROUPS = 4
C_GROUP_DIM = 64
C_WIDTH = C_GROUPS * C_GROUP_DIM
MIX_WIDTH = A_WIDTH + B_WIDTH + C_WIDTH

A_QK_COLS = A_HEADS * 2 * A_QK_DIM
B_KV_COLS = B_KV_HEADS * B_HEAD_DIM
IN_SPLITS = (A_QK_COLS, A_QK_COLS, A_WIDTH, B_WIDTH, B_KV_COLS, B_KV_COLS, C_WIDTH, C_WIDTH)
IN_COLS = sum(IN_SPLITS)
IN_OFFSETS = tuple(int(o) for o in np.cumsum(IN_SPLITS)[:-1])

FFN_HIDDEN = -(-8 * D_MODEL // (3 * 256)) * 256
DEEPNORM_ALPHA = (2 * DEPTH) ** 0.25
DEEPNORM_BETA = (8 * DEPTH) ** -0.25

kernel_name = 'hymba_style_diffattn_gqa_axial_gmlp_deepnorm_encoder'


def layer_norm(x, g, b):
    xf = x.astype(jnp.float32)
    mu = jnp.mean(xf, axis=-1, keepdims=True)
    var = jnp.mean(jnp.square(xf - mu), axis=-1, keepdims=True)
    return ((xf - mu) * lax.rsqrt(var + LN_EPS) * g.astype(jnp.float32) + b.astype(jnp.float32)).astype(x.dtype)


def rms_norm(x, g):
    xf = x.astype(jnp.float32)
    ms = jnp.mean(jnp.square(xf), axis=-1, keepdims=True)
    return (xf * lax.rsqrt(ms + LN_EPS) * g.astype(jnp.float32)).astype(x.dtype)


def rope_cos_sin(pos, dim):
    inv = 1.0 / (ROPE_THETA ** (jnp.arange(0, dim, 2, dtype=jnp.float32) / dim))
    ang = pos.astype(jnp.float32)[:, None] * inv[None, :]
    ang = jnp.concatenate([ang, ang], axis=-1)
    return jnp.cos(ang), jnp.sin(ang)


def rotate_half(x):
    x1, x2 = jnp.split(x, 2, axis=-1)
    return jnp.concatenate([-x2, x1], axis=-1)


def apply_rope(x, cos, sin):
    shape = (cos.shape[0],) + (1,) * (x.ndim - 3) + (cos.shape[1],)
    cos = cos.reshape(shape)
    sin = sin.reshape(shape)
    return (x * cos + rotate_half(x) * sin).astype(x.dtype)


def apply_axial_rope(x, cos_r, sin_r, cos_c, sin_c):
    x_row, x_col = jnp.split(x, 2, axis=-1)
    return jnp.concatenate([apply_rope(x_row, cos_r, sin_r), apply_rope(x_col, cos_c, sin_c)], axis=-1)


def sweep_query_blocks(fn, q):
    bsz, s = q.shape[:2]
    nb = s // Q_BLOCK
    qb = jnp.moveaxis(q.reshape((bsz, nb, Q_BLOCK) + q.shape[2:]), 1, 0)
    out = lax.map(fn, qb)
    out = jnp.moveaxis(out, 0, 1)
    return out.reshape((bsz, s) + out.shape[3:])


def diff_attention(q, k, v, lam):
    scale = A_QK_DIM ** -0.5

    def block(qb):
        s = jnp.einsum('bqhcd,bkhcd->bhcqk', qb, k).astype(jnp.float32) * scale
        p = jax.nn.softmax(s, axis=-1)
        w = (p[:, :, 0] - lam * p[:, :, 1]).astype(v.dtype)
        return jnp.einsum('bhqk,bkhd->bqhd', w, v)

    return sweep_query_blocks(block, q)


def gqa_attention(q, k, v):
    scale = B_HEAD_DIM ** -0.5

    def block(qb):
        s = jnp.einsum('bqhgd,bkhd->bhgqk', qb, k).astype(jnp.float32) * scale
        p = jax.nn.softmax(s, axis=-1).astype(v.dtype)
        return jnp.einsum('bhgqk,bkhd->bqhgd', p, v)

    return sweep_query_blocks(block, q)


def chunked_spatial_gating(u, v, w_s, b_s, g, beta):
    bsz, s, _ = v.shape
    v = layer_norm(v, g, beta)
    vc = v.reshape(bsz, s // CHUNK, CHUNK, C_GROUPS, C_GROUP_DIM)
    mixed = jnp.einsum('gpq,bnqgd->bnpgd', w_s, vc) + b_s.T[None, None, :, :, None]
    return u * mixed.reshape(bsz, s, C_WIDTH)


def setup_inputs(seed: int = 0) -> dict:
    key = jax.random.key(seed)
    ks = jax.random.split(key, 20)
    nrm = jax.random.normal
    f32 = jnp.float32

    def gain(k, shape):
        return 1.0 + 0.02 * nrm(k, shape, f32)

    def bias(k, shape):
        return 0.02 * nrm(k, shape, f32)

    return {
        'x': nrm(ks[0], (BATCH, SEQ, D_MODEL), f32),
        'w_in': nrm(ks[1], (DEPTH, D_MODEL, IN_COLS), f32) * D_MODEL ** -0.5,
        'w_out': nrm(ks[2], (DEPTH, MIX_WIDTH, D_MODEL), f32) * (MIX_WIDTH ** -0.5 * DEEPNORM_BETA),
        'lam_qk': 0.1 * nrm(ks[3], (DEPTH, 4, A_QK_DIM), f32),
        'a_subln_g': gain(ks[4], (DEPTH, A_V_DIM)),
        'b_q_norm_g': gain(ks[5], (DEPTH, B_HEAD_DIM)),
        'b_k_norm_g': gain(ks[6], (DEPTH, B_HEAD_DIM)),
        'c_ln_g': gain(ks[7], (DEPTH, C_WIDTH)),
        'c_ln_b': bias(ks[8], (DEPTH, C_WIDTH)),
        'c_w_s': nrm(ks[9], (DEPTH, C_GROUPS, CHUNK, CHUNK), f32) * CHUNK ** -0.5,
        'c_b_s': gain(ks[10], (DEPTH, C_GROUPS, CHUNK)),
        'ln1_g': gain(ks[11], (DEPTH, D_MODEL)),
        'ln1_b': bias(ks[12], (DEPTH, D_MODEL)),
        'w_gate': nrm(ks[13], (DEPTH, D_MODEL, FFN_HIDDEN), f32) * D_MODEL ** -0.5,
        'w_up': nrm(ks[14], (DEPTH, D_MODEL, FFN_HIDDEN), f32) * D_MODEL ** -0.5,
        'w_down': nrm(ks[15], (DEPTH, FFN_HIDDEN, D_MODEL), f32) * (FFN_HIDDEN ** -0.5 * DEEPNORM_BETA),
        'ln2_g': gain(ks[16], (DEPTH, D_MODEL)),
        'ln2_b': bias(ks[17], (DEPTH, D_MODEL)),
    }


def reference(x, w_in, w_out, lam_qk, a_subln_g, b_q_norm_g, b_k_norm_g, c_ln_g, c_ln_b,
              c_w_s, c_b_s, ln1_g, ln1_b, w_gate, w_up, w_down, ln2_g, ln2_b):
    bsz, s, _ = x.shape
    rows = s // GRID_W
    t = jnp.arange(s, dtype=jnp.int32)
    row_id = jnp.repeat(jnp.arange(rows, dtype=jnp.int32), GRID_W)
    col_id = jnp.tile(jnp.arange(GRID_W, dtype=jnp.int32), rows)
    cos_a, sin_a = rope_cos_sin(t, A_QK_DIM)
    cos_r, sin_r = rope_cos_sin(row_id, B_HEAD_DIM // 2)
    cos_c, sin_c = rope_cos_sin(col_id, B_HEAD_DIM // 2)

    for l in range(DEPTH):
        lam_init = 0.8 - 0.6 * math.exp(-0.3 * l)
        h = jnp.einsum('bsd,de->bse', x, w_in[l])
        a_q, a_k, a_v, b_q, b_k, b_v, c_u, c_v = jnp.split(h, IN_OFFSETS, axis=-1)

        a_q = apply_rope(a_q.reshape(bsz, s, A_HEADS, 2, A_QK_DIM), cos_a, sin_a)
        a_k = apply_rope(a_k.reshape(bsz, s, A_HEADS, 2, A_QK_DIM), cos_a, sin_a)
        a_v = a_v.reshape(bsz, s, A_HEADS, A_V_DIM)
        lq = lam_qk[l].astype(jnp.float32)
        lam = jnp.exp(jnp.sum(lq[0] * lq[1])) - jnp.exp(jnp.sum(lq[2] * lq[3])) + lam_init
        a_o = diff_attention(a_q, a_k, a_v, lam)
        a_o = (rms_norm(a_o, a_subln_g[l]) * (1.0 - lam_init)).reshape(bsz, s, A_WIDTH)

        b_q = rms_norm(b_q.reshape(bsz, s, B_KV_HEADS, B_GROUP, B_HEAD_DIM), b_q_norm_g[l])
        b_q = apply_axial_rope(b_q, cos_r, sin_r, cos_c, sin_c)
        b_k = rms_norm(b_k.reshape(bsz, s, B_KV_HEADS, B_HEAD_DIM), b_k_norm_g[l])
        b_k = apply_axial_rope(b_k, cos_r, sin_r, cos_c, sin_c)
        b_v = b_v.reshape(bsz, s, B_KV_HEADS, B_HEAD_DIM)
        b_o = gqa_attention(b_q, b_k, b_v).reshape(bsz, s, B_WIDTH)

        c_o = chunked_spatial_gating(jax.nn.gelu(c_u), jax.nn.gelu(c_v),
                                     c_w_s[l], c_b_s[l], c_ln_g[l], c_ln_b[l])

        mix = jnp.einsum('bse,ed->bsd', jnp.concatenate([a_o, b_o, c_o], axis=-1), w_out[l])
        x = layer_norm(DEEPNORM_ALPHA * x + mix, ln1_g[l], ln1_b[l])

        hid = jax.nn.silu(jnp.einsum('bsd,df->bsf', x, w_gate[l])) * jnp.einsum('bsd,df->bsf', x, w_up[l])
        ffn = jnp.einsum('bsf,fd->bsd', hid, w_down[l])
        x = layer_norm(DEEPNORM_ALPHA * x + ffn, ln2_g[l], ln2_b[l])
    return x
```

```python
import functools
import math

import jax
import jax.numpy as jnp
import numpy as np
from jax import lax
from jax.experimental import pallas as pl
from jax.experimental.pallas import tpu as pltpu

D_MODEL = 1024
DEPTH = 4
GRID_W = 64
ROPE_THETA = 10000.0
CHUNK = 128
LN_EPS = 1e-5

A_HEADS = 4
A_QK_DIM = 32
A_V_DIM = 64
A_WIDTH = A_HEADS * A_V_DIM
B_HEADS = 8
B_KV_HEADS = 2
B_HEAD_DIM = 64
B_GROUP = B_HEADS // B_KV_HEADS
B_WIDTH = B_HEADS * B_HEAD_DIM
B_KV_COLS = B_KV_HEADS * B_HEAD_DIM
C_GROUPS = 4
C_GROUP_DIM = 64
C_WIDTH = C_GROUPS * C_GROUP_DIM
IN_COLS = 2048
FFN_HIDDEN = 2816
DEEPNORM_ALPHA = (2 * DEPTH) ** 0.25

LANES = 128
LOG2E = 1.4426950408889634
VMEM_LIMIT_BYTES = 56 * 1024 * 1024

OFF_AQ, OFF_AK, OFF_AV = 0, 256, 512
OFF_BQ, OFF_BK, OFF_BV = 768, 1280, 1408
OFF_CU, OFF_CV = 1536, 1792

TM_IN = 512
TQ = 128
TM_POST = 256

F32 = jnp.float32
BF16 = jnp.bfloat16


def _bq_permutation():
    n = np.arange(B_WIDTH)
    head = n // LANES + B_GROUP * ((n % LANES) // B_HEAD_DIM)
    return head * B_HEAD_DIM + n % B_HEAD_DIM


_BQ_PERM = _bq_permutation()


def _gelu_tanh(x):
    c = math.sqrt(2.0 / math.pi)
    return x * (0.5 * (1.0 + jnp.tanh(c * (x + 0.044715 * (x * x * x)))))


def _layer_norm(x, g, b):
    mu = jnp.mean(x, axis=-1, keepdims=True)
    xc = x - mu
    var = jnp.mean(xc * xc, axis=-1, keepdims=True)
    return xc * lax.rsqrt(var + LN_EPS) * g + b


def _rope128(xb, cos, sin_lo, sin_hi):
    return xb * cos + pltpu.roll(xb, LANES - 16, 1) * sin_lo + pltpu.roll(xb, 16, 1) * sin_hi


def _rms64(xb, lo, g128, post_scale):
    x2 = xb * xb
    s_lo = jnp.sum(jnp.where(lo, x2, 0.0), axis=-1, keepdims=True)
    s_hi = jnp.sum(jnp.where(lo, 0.0, x2), axis=-1, keepdims=True)
    r = lax.rsqrt(jnp.where(lo, s_lo, s_hi) * (1.0 / B_HEAD_DIM) + LN_EPS)
    return xb * r * g128 * post_scale


def _in_kernel(x_ref, w_ref, ta_ref, tb_ref, gq_ref, gk_ref, clg_ref, clb_ref, ws_ref, bsm_ref,
               aq_ref, ak_ref, av_ref, bq_ref, bk_ref, bv_ref, co_ref, *, tm, blocks_per_seq):
    i = pl.program_id(0)
    t0 = pl.multiple_of((i % blocks_per_seq) * tm, tm)
    xb = x_ref[...].astype(BF16)

    def proj(off, width):
        return jnp.dot(xb, w_ref[:, off:off + width], preferred_element_type=F32)

    rows = pl.ds(t0, tm)
    a_cos, a_slo, a_shi = ta_ref[0, rows, :], ta_ref[1, rows, :], ta_ref[2, rows, :]
    b_cos, b_slo, b_shi = tb_ref[0, rows, :], tb_ref[1, rows, :], tb_ref[2, rows, :]
    lane = lax.broadcasted_iota(jnp.int32, (tm, LANES), 1)
    lo = lane < B_HEAD_DIM

    a_scale = A_QK_DIM ** -0.5 * LOG2E
    hq = proj(OFF_AQ, 256)
    hk = proj(OFF_AK, 256)
    for c in range(2):
        sl = slice(c * LANES, (c + 1) * LANES)
        aq_ref[:, sl] = (_rope128(hq[:, sl], a_cos, a_slo, a_shi) * a_scale).astype(BF16)
        ak_ref[:, sl] = _rope128(hk[:, sl], a_cos, a_slo, a_shi).astype(BF16)
    av_ref[...] = proj(OFF_AV, 256).astype(BF16)

    b_scale = B_HEAD_DIM ** -0.5 * LOG2E
    hq = proj(OFF_BQ, B_WIDTH)
    gq = gq_ref[...]
    for c in range(B_WIDTH // LANES):
        sl = slice(c * LANES, (c + 1) * LANES)
        qn = _rms64(hq[:, sl], lo, gq, b_scale)
        bq_ref[:, sl] = _rope128(qn, b_cos, b_slo, b_shi).astype(BF16)
    kn = _rms64(proj(OFF_BK, B_KV_COLS), lo, gk_ref[...], 1.0)
    bk_ref[...] = _rope128(kn, b_cos, b_slo, b_shi).astype(BF16)
    bv_ref[...] = proj(OFF_BV, B_KV_COLS).astype(BF16)

    cu = _gelu_tanh(proj(OFF_CU, C_WIDTH))
    cv = _layer_norm(_gelu_tanh(proj(OFF_CV, C_WIDTH)), clg_ref[...], clb_ref[...]).astype(BF16)
    group = lax.broadcasted_iota(jnp.int32, (CHUNK, C_WIDTH), 1) // C_GROUP_DIM
    bsm = bsm_ref[...]
    for ci in range(tm // CHUNK):
        rs = slice(ci * CHUNK, (ci + 1) * CHUNK)
        vch = cv[rs, :]
        mixed = bsm
        for g in range(C_GROUPS):
            m = jnp.dot(ws_ref[g], vch, preferred_element_type=F32)
            mixed = mixed + jnp.where(group == g, m, 0.0)
        co_ref[rs, :] = (cu[rs, :] * mixed).astype(BF16)


def _in_call(x2d, w_in, ta, tb, gq, gk, clg, clb, ws, bsm, seq):
    m = x2d.shape[0]
    tm = TM_IN
    const2 = lambda i: (0, 0)
    const3 = lambda i: (0, 0, 0)
    row = lambda i: (i, 0)
    out_widths = (256, 256, 256, B_WIDTH, B_KV_COLS, B_KV_COLS, C_WIDTH)
    return pl.pallas_call(
        functools.partial(_in_kernel, tm=tm, blocks_per_seq=seq // tm),
        grid=(m // tm,),
        in_specs=[
            pl.BlockSpec((tm, D_MODEL), row),
            pl.BlockSpec((D_MODEL, IN_COLS), const2),
            pl.BlockSpec((3, seq, LANES), const3),
            pl.BlockSpec((3, seq, LANES), const3),
            pl.BlockSpec((1, LANES), const2),
            pl.BlockSpec((1, LANES), const2),
            pl.BlockSpec((1, C_WIDTH), const2),
            pl.BlockSpec((1, C_WIDTH), const2),
            pl.BlockSpec((C_GROUPS, CHUNK, CHUNK), const3),
            pl.BlockSpec((CHUNK, C_WIDTH), const2),
        ],
        out_specs=[pl.BlockSpec((tm, w), row) for w in out_widths],
        out_shape=[jax.ShapeDtypeStruct((m, w), BF16) for w in out_widths],
        compiler_params=pltpu.CompilerParams(
            dimension_semantics=("arbitrary",), vmem_limit_bytes=VMEM_LIMIT_BYTES),
        name="in_proj",
    )(x2d, w_in, ta, tb, gq, gk, clg, clb, ws, bsm)


def _softmax_parts(qm, k):
    s = lax.dot_general(qm, k, (((1,), (1,)), ((), ())), preferred_element_type=F32)
    m = jnp.max(s, axis=-1, keepdims=True)
    e = jnp.exp2(s - m)
    return e, jnp.sum(e, axis=-1, keepdims=True)


def _attn_a_kernel(lq_ref, g_ref, q_ref, k_ref, v_ref, o_ref, *, lam_init):
    lq = lq_ref[...]
    lam = (jnp.exp(jnp.sum(lq[0:1] * lq[1:2], axis=-1, keepdims=True))
           - jnp.exp(jnp.sum(lq[2:3] * lq[3:4], axis=-1, keepdims=True)) + lam_init)
    q = q_ref[...]
    k = k_ref[...]
    v = v_ref[...]
    tq = q.shape[0]
    lane = lax.broadcasted_iota(jnp.int32, (tq, A_WIDTH), 1)
    zero = jnp.zeros_like(q)
    acc = jnp.zeros((tq, A_WIDTH), F32)
    for h in range(A_HEADS):
        base = h * A_V_DIM
        parts = []
        for c in range(2):
            lo_ = base + c * A_QK_DIM
            qm = jnp.where((lane >= lo_) & (lane < lo_ + A_QK_DIM), q, zero)
            parts.append(_softmax_parts(qm, k))
        (e1, l1), (e2, l2) = parts
        w = (e1 - (lam * l1 / l2) * e2).astype(BF16)
        o = jnp.dot(w, v, preferred_element_type=F32) * (1.0 / l1)
        acc = jnp.where((lane >= base) & (lane < base + A_V_DIM), o, acc)
    a2 = acc * acc
    r = jnp.zeros((tq, A_WIDTH), F32)
    for h in range(A_HEADS):
        hm = (lane >= h * A_V_DIM) & (lane < (h + 1) * A_V_DIM)
        ms = jnp.sum(jnp.where(hm, a2, 0.0), axis=-1, keepdims=True) * (1.0 / A_V_DIM)
        r = jnp.where(hm, lax.rsqrt(ms + LN_EPS), r)
    o_ref[...] = (acc * r * g_ref[...] * (1.0 - lam_init)).astype(BF16)


def _attn_a_call(lq, g256, aq, ak, av, seq, lam_init):
    m = aq.shape[0]
    nq = seq // TQ
    const2 = lambda b, i: (0, 0)
    return pl.pallas_call(
        functools.partial(_attn_a_kernel, lam_init=lam_init),
        grid=(m // seq, nq),
        in_specs=[
            pl.BlockSpec((4, A_QK_DIM), const2),
            pl.BlockSpec((1, A_WIDTH), const2),
            pl.BlockSpec((TQ, A_WIDTH), lambda b, i: (b * nq + i, 0)),
            pl.BlockSpec((seq, A_WIDTH), lambda b, i: (b, 0)),
            pl.BlockSpec((seq, A_WIDTH), lambda b, i: (b, 0)),
        ],
        out_specs=pl.BlockSpec((TQ, A_WIDTH), lambda b, i: (b * nq + i, 0)),
        out_shape=jax.ShapeDtypeStruct((m, A_WIDTH), BF16),
        compiler_params=pltpu.CompilerParams(
            dimension_semantics=("arbitrary", "arbitrary"), vmem_limit_bytes=VMEM_LIMIT_BYTES),
        name="attn_a",
    )(lq, g256, aq, ak, av)


def _attn_b_kernel(q_ref, k_ref, v_ref, o_ref):
    k = k_ref[...]
    v = v_ref[...]
    tq = q_ref.shape[0]
    lo = lax.broadcasted_iota(jnp.int32, (tq, LANES), 1) < B_HEAD_DIM
    for c in range(B_WIDTH // LANES):
        sl = slice(c * LANES, (c + 1) * LANES)
        qc = q_ref[:, sl]
        zero = jnp.zeros_like(qc)
        outs = []
        for half in range(2):
            qm = jnp.where(lo, qc, zero) if half == 0 else jnp.where(lo, zero, qc)
            e, l = _softmax_parts(qm, k)
            outs.append(jnp.dot(e.astype(BF16), v, preferred_element_type=F32) * (1.0 / l))
        o_ref[:, sl] = jnp.where(lo, outs[0], outs[1]).astype(BF16)


def _attn_b_call(bq, bk, bv, seq):
    m = bq.shape[0]
    nq = seq // TQ
    return pl.pallas_call(
        _attn_b_kernel,
        grid=(m // seq, nq),
        in_specs=[
            pl.BlockSpec((TQ, B_WIDTH), lambda b, i: (b * nq + i, 0)),
            pl.BlockSpec((seq, B_KV_COLS), lambda b, i: (b, 0)),
            pl.BlockSpec((seq, B_KV_COLS), lambda b, i: (b, 0)),
        ],
        out_specs=pl.BlockSpec((TQ, B_WIDTH), lambda b, i: (b * nq + i, 0)),
        out_shape=jax.ShapeDtypeStruct((m, B_WIDTH), BF16),
        compiler_params=pltpu.CompilerParams(
            dimension_semantics=("arbitrary", "arbitrary"), vmem_limit_bytes=VMEM_LIMIT_BYTES),
        name="attn_b",
    )(bq, bk, bv)


def _post_kernel(x_ref, a_ref, b_ref, c_ref, wa_ref, wb_ref, wc_ref, g1_ref, b1_ref,
                 wg_ref, wu_ref, wd_ref, g2_ref, b2_ref, o_ref):
    mix = (jnp.dot(a_ref[...], wa_ref[...], preferred_element_type=F32)
           + jnp.dot(b_ref[...], wb_ref[...], preferred_element_type=F32)
           + jnp.dot(c_ref[...], wc_ref[...], preferred_element_type=F32))
    y = _layer_norm(DEEPNORM_ALPHA * x_ref[...] + mix, g1_ref[...], b1_ref[...])
    yb = y.astype(BF16)
    gate = jnp.dot(yb, wg_ref[...], preferred_element_type=F32)
    up = jnp.dot(yb, wu_ref[...], preferred_element_type=F32)
    hid = (gate * jax.nn.sigmoid(gate) * up).astype(BF16)
    ffn = jnp.dot(hid, wd_ref[...], preferred_element_type=F32)
    o_ref[...] = _layer_norm(DEEPNORM_ALPHA * y + ffn, g2_ref[...], b2_ref[...])


def _post_call(x2d, ao, bo, co, wa, wb, wc, g1, b1, wg, wu, wd, g2, b2):
    m = x2d.shape[0]
    tm = TM_POST
    row = lambda i: (i, 0)
    const2 = lambda i: (0, 0)

    def resident(shape):
        return pl.BlockSpec(shape, const2, pipeline_mode=pl.Buffered(1))

    return pl.pallas_call(
        _post_kernel,
        grid=(m // tm,),
        in_specs=[
            pl.BlockSpec((tm, D_MODEL), row),
            pl.BlockSpec((tm, A_WIDTH), row),
            pl.BlockSpec((tm, B_WIDTH), row),
            pl.BlockSpec((tm, C_WIDTH), row),
            resident((A_WIDTH, D_MODEL)),
            resident((B_WIDTH, D_MODEL)),
            resident((C_WIDTH, D_MODEL)),
            resident((1, D_MODEL)),
            resident((1, D_MODEL)),
            resident((D_MODEL, FFN_HIDDEN)),
            resident((D_MODEL, FFN_HIDDEN)),
            resident((FFN_HIDDEN, D_MODEL)),
            resident((1, D_MODEL)),
            resident((1, D_MODEL)),
        ],
        out_specs=pl.BlockSpec((tm, D_MODEL), row),
        out_shape=jax.ShapeDtypeStruct((m, D_MODEL), F32),
        compiler_params=pltpu.CompilerParams(
            dimension_semantics=("arbitrary",), vmem_limit_bytes=VMEM_LIMIT_BYTES),
        name="post",
    )(x2d, ao, bo, co, wa, wb, wc, g1, b1, wg, wu, wd, g2, b2)


def _rope_tables(pos, dim):
    inv = 1.0 / (ROPE_THETA ** (jnp.arange(0, dim, 2, dtype=F32) / dim))
    ang = pos.astype(F32)[:, None] * inv[None, :]
    ang = jnp.concatenate([ang, ang], axis=-1)
    return jnp.cos(ang), jnp.sin(ang)


def _signed_tables(cos, sin):
    first = (jnp.arange(LANES) % 32) < 16
    return jnp.stack([cos, jnp.where(first, -sin, 0.0), jnp.where(first, 0.0, sin)])


def kernel(x, w_in, w_out, lam_qk, a_subln_g, b_q_norm_g, b_k_norm_g, c_ln_g, c_ln_b, c_w_s, c_b_s,
           ln1_g, ln1_b, w_gate, w_up, w_down, ln2_g, ln2_b):
    bsz, seq, _ = x.shape
    assert seq % TM_IN == 0 and seq % TQ == 0 and seq % GRID_W == 0
    t = jnp.arange(seq, dtype=jnp.int32)
    cos_a, sin_a = _rope_tables(t, A_QK_DIM)
    cos_r, sin_r = _rope_tables(t // GRID_W, B_HEAD_DIM // 2)
    cos_c, sin_c = _rope_tables(t % GRID_W, B_HEAD_DIM // 2)
    ta = _signed_tables(jnp.tile(cos_a, (1, 4)), jnp.tile(sin_a, (1, 4)))
    tb = _signed_tables(jnp.tile(jnp.concatenate([cos_r, cos_c], -1), (1, 2)),
                        jnp.tile(jnp.concatenate([sin_r, sin_c], -1), (1, 2)))

    in_perm = np.arange(IN_COLS)
    in_perm[OFF_BQ:OFF_BQ + B_WIDTH] = OFF_BQ + _BQ_PERM
    x2d = x.reshape(bsz * seq, D_MODEL)
    for l in range(DEPTH):
        lam_init = 0.8 - 0.6 * math.exp(-0.3 * l)
        w_in_l = w_in[l][:, in_perm].astype(BF16)
        w_out_l = w_out[l].astype(BF16)
        wa = w_out_l[:A_WIDTH]
        wb = w_out_l[A_WIDTH:A_WIDTH + B_WIDTH][_BQ_PERM]
        wc = w_out_l[A_WIDTH + B_WIDTH:]
        bsm = jnp.repeat(c_b_s[l].T, C_GROUP_DIM, axis=1)
        aq, ak, av, bq, bk, bv, co = _in_call(
            x2d, w_in_l, ta, tb,
            jnp.tile(b_q_norm_g[l], 2)[None], jnp.tile(b_k_norm_g[l], 2)[None],
            c_ln_g[l][None], c_ln_b[l][None], c_w_s[l].astype(BF16), bsm, seq)
        ao = _attn_a_call(lam_qk[l], jnp.tile(a_subln_g[l], A_HEADS)[None], aq, ak, av, seq, lam_init)
        bo = _attn_b_call(bq, bk, bv, seq)
        x2d = _post_call(x2d, ao, bo, co, wa, wb, wc, ln1_g[l][None], ln1_b[l][None],
                         w_gate[l].astype(BF16), w_up[l].astype(BF16), w_down[l].astype(BF16),
                         ln2_g[l][None], ln2_b[l][None])
    return x2d.reshape(bsz, seq, D_MODEL)
```

```python
import functools
import math

import jax
import jax.numpy as jnp
import numpy as np
from jax import lax
from jax.experimental import pallas as pl
from jax.experimental.pallas import tpu as pltpu

D_MODEL = 1024
DEPTH = 4
GRID_W = 64
ROPE_THETA = 10000.0
CHUNK = 128
LN_EPS = 1e-5

A_HEADS = 4
A_QK_DIM = 32
A_V_DIM = 64
A_WIDTH = A_HEADS * A_V_DIM
B_HEADS = 8
B_KV_HEADS = 2
B_HEAD_DIM = 64
B_GROUP = B_HEADS // B_KV_HEADS
B_WIDTH = B_HEADS * B_HEAD_DIM
B_KV_COLS = B_KV_HEADS * B_HEAD_DIM
C_GROUPS = 4
C_GROUP_DIM = 64
C_WIDTH = C_GROUPS * C_GROUP_DIM
IN_COLS = 2048
FFN_HIDDEN = 2816
DEEPNORM_ALPHA = (2 * DEPTH) ** 0.25

LANES = 128
LOG2E = 1.4426950408889634
VMEM_LIMIT_BYTES = 56 * 1024 * 1024

OFF_AQ, OFF_AK, OFF_AV = 0, 256, 512
OFF_BQ, OFF_BK, OFF_BV = 768, 1280, 1408
OFF_CU, OFF_CV = 1536, 1792

TM_IN = 512
TQ = 256
TM_POST = 256

F32 = jnp.float32
BF16 = jnp.bfloat16


def _bq_permutation():
    n = np.arange(B_WIDTH)
    head = n // LANES + B_GROUP * ((n % LANES) // B_HEAD_DIM)
    return head * B_HEAD_DIM + n % B_HEAD_DIM


_BQ_PERM = _bq_permutation()


def _gelu_tanh(x):
    c = math.sqrt(2.0 / math.pi)
    return x * (0.5 * (1.0 + jnp.tanh(c * (x + 0.044715 * (x * x * x)))))


def _layer_norm(x, g, b):
    mu = jnp.mean(x, axis=-1, keepdims=True)
    xc = x - mu
    var = jnp.mean(xc * xc, axis=-1, keepdims=True)
    return xc * lax.rsqrt(var + LN_EPS) * g + b


def _rope128(xb, cos, sin_lo, sin_hi):
    return xb * cos + pltpu.roll(xb, LANES - 16, 1) * sin_lo + pltpu.roll(xb, 16, 1) * sin_hi


def _rms64(xb, lo, g128, post_scale):
    x2 = xb * xb
    s_lo = jnp.sum(jnp.where(lo, x2, 0.0), axis=-1, keepdims=True)
    s_hi = jnp.sum(jnp.where(lo, 0.0, x2), axis=-1, keepdims=True)
    r = lax.rsqrt(jnp.where(lo, s_lo, s_hi) * (1.0 / B_HEAD_DIM) + LN_EPS)
    return xb * r * g128 * post_scale


def _in_kernel(x_ref, w_ref, ta_ref, tb_ref, gq_ref, gk_ref, clg_ref, clb_ref, ws_ref, bsm_ref,
               aq_ref, akt_ref, av_ref, bq_ref, bkt_ref, bv_ref, co_ref, *, tm, blocks_per_seq):
    i = pl.program_id(0)
    t0 = pl.multiple_of((i % blocks_per_seq) * tm, tm)
    xb = x_ref[...].astype(BF16)

    def proj(off, width):
        return jnp.dot(xb, w_ref[:, off:off + width], preferred_element_type=F32)

    rows = pl.ds(t0, tm)
    a_cos, a_slo, a_shi = ta_ref[0, rows, :], ta_ref[1, rows, :], ta_ref[2, rows, :]
    b_cos, b_slo, b_shi = tb_ref[0, rows, :], tb_ref[1, rows, :], tb_ref[2, rows, :]
    lane = lax.broadcasted_iota(jnp.int32, (tm, LANES), 1)
    lo = lane < B_HEAD_DIM

    a_scale = A_QK_DIM ** -0.5 * LOG2E
    hq = proj(OFF_AQ, 256)
    hk = proj(OFF_AK, 256)
    for c in range(2):
        sl = slice(c * LANES, (c + 1) * LANES)
        aq_ref[:, sl] = (_rope128(hq[:, sl], a_cos, a_slo, a_shi) * a_scale).astype(BF16)
        akt_ref[sl, :] = _rope128(hk[:, sl], a_cos, a_slo, a_shi).T.astype(BF16)
    av_ref[...] = proj(OFF_AV, 256).astype(BF16)

    b_scale = B_HEAD_DIM ** -0.5 * LOG2E
    hq = proj(OFF_BQ, B_WIDTH)
    gq = gq_ref[...]
    for c in range(B_WIDTH // LANES):
        sl = slice(c * LANES, (c + 1) * LANES)
        qn = _rms64(hq[:, sl], lo, gq, b_scale)
        bq_ref[:, sl] = _rope128(qn, b_cos, b_slo, b_shi).astype(BF16)
    kn = _rms64(proj(OFF_BK, B_KV_COLS), lo, gk_ref[...], 1.0)
    bkt_ref[...] = _rope128(kn, b_cos, b_slo, b_shi).T.astype(BF16)
    bv_ref[...] = proj(OFF_BV, B_KV_COLS).astype(BF16)

    cu = _gelu_tanh(proj(OFF_CU, C_WIDTH))
    cv = _layer_norm(_gelu_tanh(proj(OFF_CV, C_WIDTH)), clg_ref[...], clb_ref[...]).astype(BF16)
    group = lax.broadcasted_iota(jnp.int32, (CHUNK, C_WIDTH), 1) // C_GROUP_DIM
    bsm = bsm_ref[...]
    for ci in range(tm // CHUNK):
        rs = slice(ci * CHUNK, (ci + 1) * CHUNK)
        vch = cv[rs, :]
        mixed = bsm
        for g in range(C_GROUPS):
            m = jnp.dot(ws_ref[g], vch, preferred_element_type=F32)
            mixed = mixed + jnp.where(group == g, m, 0.0)
        co_ref[rs, :] = (cu[rs, :] * mixed).astype(BF16)


def _in_call(x2d, w_in, ta, tb, gq, gk, clg, clb, ws, bsm, seq):
    m = x2d.shape[0]
    tm = TM_IN
    const2 = lambda i: (0, 0)
    const3 = lambda i: (0, 0, 0)
    row = lambda i: (i, 0)
    outs = ((256, False), (256, True), (256, False), (B_WIDTH, False), (B_KV_COLS, True),
            (B_KV_COLS, False), (C_WIDTH, False))
    col = lambda i: (0, i)
    return pl.pallas_call(
        functools.partial(_in_kernel, tm=tm, blocks_per_seq=seq // tm),
        grid=(m // tm,),
        in_specs=[
            pl.BlockSpec((tm, D_MODEL), row),
            pl.BlockSpec((D_MODEL, IN_COLS), const2),
            pl.BlockSpec((3, seq, LANES), const3),
            pl.BlockSpec((3, seq, LANES), const3),
            pl.BlockSpec((1, LANES), const2),
            pl.BlockSpec((1, LANES), const2),
            pl.BlockSpec((1, C_WIDTH), const2),
            pl.BlockSpec((1, C_WIDTH), const2),
            pl.BlockSpec((C_GROUPS, CHUNK, CHUNK), const3),
            pl.BlockSpec((CHUNK, C_WIDTH), const2),
        ],
        out_specs=[pl.BlockSpec((w, tm), col) if t else pl.BlockSpec((tm, w), row) for w, t in outs],
        out_shape=[jax.ShapeDtypeStruct((w, m) if t else (m, w), BF16) for w, t in outs],
        compiler_params=pltpu.CompilerParams(
            dimension_semantics=("arbitrary",), vmem_limit_bytes=VMEM_LIMIT_BYTES),
        name="in_proj",
    )(x2d, w_in, ta, tb, gq, gk, clg, clb, ws, bsm)


def _qk_scores(lhs, kt_ref, row0, s_ref):
    s_ref[...] = jnp.dot(lhs, kt_ref[row0:row0 + LANES, :], preferred_element_type=F32)


def _softmax_numerator(s_ref):
    s = s_ref[...]
    e = jnp.exp2(s - jnp.max(s, axis=-1, keepdims=True))
    return e, jnp.sum(e, axis=-1, keepdims=True)


def _attn_a_kernel(lq_ref, g_ref, q_ref, kt_ref, v_ref, o_ref, s0_ref, s1_ref, w0_ref, w1_ref, *, lam_init):
    lq = lq_ref[...]
    lam = (jnp.exp(jnp.sum(lq[0:1] * lq[1:2], axis=-1, keepdims=True))
           - jnp.exp(jnp.sum(lq[2:3] * lq[3:4], axis=-1, keepdims=True)) + lam_init)
    tq = q_ref.shape[0]
    lane = lax.broadcasted_iota(jnp.int32, (tq, LANES), 1)
    s_refs = (s0_ref, s1_ref)
    w_refs = (w0_ref, w1_ref)

    def scores(h):
        col0 = (h // 2) * LANES
        qc = q_ref[:, col0:col0 + LANES]
        zero = jnp.zeros_like(qc)
        base = (h % 2) * A_V_DIM
        lhs = jnp.concatenate(
            [jnp.where((lane >= base + c * A_QK_DIM) & (lane < base + (c + 1) * A_QK_DIM), qc, zero)
             for c in range(2)], axis=0)
        _qk_scores(lhs, kt_ref, col0, s_refs[h % 2])

    def combine(h):
        e, l = _softmax_numerator(s_refs[h % 2])
        l1, l2 = l[:tq], l[tq:]
        w_refs[h % 2][...] = (e[:tq] - (lam * l1 / l2) * e[tq:]).astype(BF16)
        return 1.0 / l1

    def values(h, inv_l1):
        col0 = (h // 2) * LANES
        return jnp.dot(w_refs[h % 2][...], v_ref[:, col0:col0 + LANES], preferred_element_type=F32) * inv_l1

    lo = lane < A_V_DIM
    outs = []
    scores(0)
    for h in range(A_HEADS):
        if h + 1 < A_HEADS:
            scores(h + 1)
        outs.append(values(h, combine(h)))
    for cb in range(A_WIDTH // LANES):
        o = jnp.where(lo, outs[2 * cb], outs[2 * cb + 1])
        o2 = o * o
        ms_lo = jnp.sum(jnp.where(lo, o2, 0.0), axis=-1, keepdims=True)
        ms_hi = jnp.sum(jnp.where(lo, 0.0, o2), axis=-1, keepdims=True)
        r = lax.rsqrt(jnp.where(lo, ms_lo, ms_hi) * (1.0 / A_V_DIM) + LN_EPS)
        o_ref[:, cb * LANES:(cb + 1) * LANES] = (o * r * g_ref[...] * (1.0 - lam_init)).astype(BF16)


def _attn_a_call(lq, g128, aq, akt, av, seq, lam_init):
    m = aq.shape[0]
    nq = seq // TQ
    const2 = lambda b, i: (0, 0)
    return pl.pallas_call(
        functools.partial(_attn_a_kernel, lam_init=lam_init),
        grid=(m // seq, nq),
        in_specs=[
            pl.BlockSpec((4, A_QK_DIM), const2),
            pl.BlockSpec((1, LANES), const2),
            pl.BlockSpec((TQ, A_WIDTH), lambda b, i: (b * nq + i, 0)),
            pl.BlockSpec((A_WIDTH, seq), lambda b, i: (0, b)),
            pl.BlockSpec((seq, A_WIDTH), lambda b, i: (b, 0)),
        ],
        out_specs=pl.BlockSpec((TQ, A_WIDTH), lambda b, i: (b * nq + i, 0)),
        out_shape=jax.ShapeDtypeStruct((m, A_WIDTH), BF16),
        scratch_shapes=[pltpu.VMEM((2 * TQ, seq), F32), pltpu.VMEM((2 * TQ, seq), F32),
                        pltpu.VMEM((TQ, seq), BF16), pltpu.VMEM((TQ, seq), BF16)],
        compiler_params=pltpu.CompilerParams(
            dimension_semantics=("arbitrary", "arbitrary"), vmem_limit_bytes=VMEM_LIMIT_BYTES),
        name="attn_a",
    )(lq, g128, aq, akt, av)


def _attn_b_kernel(q_ref, kt_ref, v_ref, o_ref, s0_ref, s1_ref, e0_ref, e1_ref):
    tq = q_ref.shape[0]
    lo = lax.broadcasted_iota(jnp.int32, (tq, LANES), 1) < B_HEAD_DIM
    s_refs = (s0_ref, s1_ref)
    e_refs = (e0_ref, e1_ref)
    n_units = B_WIDTH // LANES

    def scores(c):
        qc = q_ref[:, c * LANES:(c + 1) * LANES]
        zero = jnp.zeros_like(qc)
        lhs = jnp.concatenate([jnp.where(lo, qc, zero), jnp.where(lo, zero, qc)], axis=0)
        _qk_scores(lhs, kt_ref, 0, s_refs[c % 2])

    def probs(c):
        e, l = _softmax_numerator(s_refs[c % 2])
        e_refs[c % 2][...] = e.astype(BF16)
        return 1.0 / l

    def values(c, inv_l):
        o = jnp.dot(e_refs[c % 2][...], v_ref[...], preferred_element_type=F32) * inv_l
        o_ref[:, c * LANES:(c + 1) * LANES] = jnp.where(lo, o[:tq], o[tq:]).astype(BF16)

    scores(0)
    for c in range(n_units):
        if c + 1 < n_units:
            scores(c + 1)
        values(c, probs(c))


def _attn_b_call(bq, bkt, bv, seq):
    m = bq.shape[0]
    nq = seq // TQ
    return pl.pallas_call(
        _attn_b_kernel,
        grid=(m // seq, nq),
        in_specs=[
            pl.BlockSpec((TQ, B_WIDTH), lambda b, i: (b * nq + i, 0)),
            pl.BlockSpec((B_KV_COLS, seq), lambda b, i: (0, b)),
            pl.BlockSpec((seq, B_KV_COLS), lambda b, i: (b, 0)),
        ],
        out_specs=pl.BlockSpec((TQ, B_WIDTH), lambda b, i: (b * nq + i, 0)),
        out_shape=jax.ShapeDtypeStruct((m, B_WIDTH), BF16),
        scratch_shapes=[pltpu.VMEM((2 * TQ, seq), F32), pltpu.VMEM((2 * TQ, seq), F32),
                        pltpu.VMEM((2 * TQ, seq), BF16), pltpu.VMEM((2 * TQ, seq), BF16)],
        compiler_params=pltpu.CompilerParams(
            dimension_semantics=("arbitrary", "arbitrary"), vmem_limit_bytes=VMEM_LIMIT_BYTES),
        name="attn_b",
    )(bq, bkt, bv)


def _post_kernel(x_ref, a_ref, b_ref, c_ref, wa_ref, wb_ref, wc_ref, g1_ref, b1_ref,
                 wg_ref, wu_ref, wd_ref, g2_ref, b2_ref, o_ref):
    mix = (jnp.dot(a_ref[...], wa_ref[...], preferred_element_type=F32)
           + jnp.dot(b_ref[...], wb_ref[...], preferred_element_type=F32)
           + jnp.dot(c_ref[...], wc_ref[...], preferred_element_type=F32))
    y = _layer_norm(DEEPNORM_ALPHA * x_ref[...] + mix, g1_ref[...], b1_ref[...])
    yb = y.astype(BF16)
    gate = jnp.dot(yb, wg_ref[...], preferred_element_type=F32)
    up = jnp.dot(yb, wu_ref[...], preferred_element_type=F32)
    hid = (gate * jax.nn.sigmoid(gate) * up).astype(BF16)
    ffn = jnp.dot(hid, wd_ref[...], preferred_element_type=F32)
    o_ref[...] = _layer_norm(DEEPNORM_ALPHA * y + ffn, g2_ref[...], b2_ref[...])


def _post_call(x2d, ao, bo, co, wa, wb, wc, g1, b1, wg, wu, wd, g2, b2):
    m = x2d.shape[0]
    tm = TM_POST
    row = lambda i: (i, 0)
    const2 = lambda i: (0, 0)

    def resident(shape):
        return pl.BlockSpec(shape, const2, pipeline_mode=pl.Buffered(1))

    return pl.pallas_call(
        _post_kernel,
        grid=(m // tm,),
        in_specs=[
            pl.BlockSpec((tm, D_MODEL), row),
            pl.BlockSpec((tm, A_WIDTH), row),
            pl.BlockSpec((tm, B_WIDTH), row),
            pl.BlockSpec((tm, C_WIDTH), row),
            resident((A_WIDTH, D_MODEL)),
            resident((B_WIDTH, D_MODEL)),
            resident((C_WIDTH, D_MODEL)),
            resident((1, D_MODEL)),
            resident((1, D_MODEL)),
            resident((D_MODEL, FFN_HIDDEN)),
            resident((D_MODEL, FFN_HIDDEN)),
            resident((FFN_HIDDEN, D_MODEL)),
            resident((1, D_MODEL)),
            resident((1, D_MODEL)),
        ],
        out_specs=pl.BlockSpec((tm, D_MODEL), row),
        out_shape=jax.ShapeDtypeStruct((m, D_MODEL), F32),
        compiler_params=pltpu.CompilerParams(
            dimension_semantics=("arbitrary",), vmem_limit_bytes=VMEM_LIMIT_BYTES),
        name="post",
    )(x2d, ao, bo, co, wa, wb, wc, g1, b1, wg, wu, wd, g2, b2)


def _rope_tables(pos, dim):
    inv = 1.0 / (ROPE_THETA ** (jnp.arange(0, dim, 2, dtype=F32) / dim))
    ang = pos.astype(F32)[:, None] * inv[None, :]
    ang = jnp.concatenate([ang, ang], axis=-1)
    return jnp.cos(ang), jnp.sin(ang)


def _signed_tables(cos, sin):
    first = (jnp.arange(LANES) % 32) < 16
    return jnp.stack([cos, jnp.where(first, -sin, 0.0), jnp.where(first, 0.0, sin)])


def kernel(x, w_in, w_out, lam_qk, a_subln_g, b_q_norm_g, b_k_norm_g, c_ln_g, c_ln_b, c_w_s, c_b_s,
           ln1_g, ln1_b, w_gate, w_up, w_down, ln2_g, ln2_b):
    bsz, seq, _ = x.shape
    assert seq % TM_IN == 0 and seq % TQ == 0 and seq % GRID_W == 0
    t = jnp.arange(seq, dtype=jnp.int32)
    cos_a, sin_a = _rope_tables(t, A_QK_DIM)
    cos_r, sin_r = _rope_tables(t // GRID_W, B_HEAD_DIM // 2)
    cos_c, sin_c = _rope_tables(t % GRID_W, B_HEAD_DIM // 2)
    ta = _signed_tables(jnp.tile(cos_a, (1, 4)), jnp.tile(sin_a, (1, 4)))
    tb = _signed_tables(jnp.tile(jnp.concatenate([cos_r, cos_c], -1), (1, 2)),
                        jnp.tile(jnp.concatenate([sin_r, sin_c], -1), (1, 2)))

    in_perm = np.arange(IN_COLS)
    in_perm[OFF_BQ:OFF_BQ + B_WIDTH] = OFF_BQ + _BQ_PERM
    x2d = x.reshape(bsz * seq, D_MODEL)
    for l in range(DEPTH):
        lam_init = 0.8 - 0.6 * math.exp(-0.3 * l)
        w_in_l = w_in[l][:, in_perm].astype(BF16)
        w_out_l = w_out[l].astype(BF16)
        wa = w_out_l[:A_WIDTH]
        wb = w_out_l[A_WIDTH:A_WIDTH + B_WIDTH][_BQ_PERM]
        wc = w_out_l[A_WIDTH + B_WIDTH:]
        bsm = jnp.repeat(c_b_s[l].T, C_GROUP_DIM, axis=1)
        aq, akt, av, bq, bkt, bv, co = _in_call(
            x2d, w_in_l, ta, tb,
            jnp.tile(b_q_norm_g[l], 2)[None], jnp.tile(b_k_norm_g[l], 2)[None],
            c_ln_g[l][None], c_ln_b[l][None], c_w_s[l].astype(BF16), bsm, seq)
        ao = _attn_a_call(lam_qk[l], jnp.tile(a_subln_g[l], 2)[None], aq, akt, av, seq, lam_init)
        bo = _attn_b_call(bq, bkt, bv, seq)
        x2d = _post_call(x2d, ao, bo, co, wa, wb, wc, ln1_g[l][None], ln1_b[l][None],
                         w_gate[l].astype(BF16), w_up[l].astype(BF16), w_down[l].astype(BF16),
                         ln2_g[l][None], ln2_b[l][None])
    return x2d.reshape(bsz, seq, D_MODEL)
```

```python
import functools
import math

import jax
import jax.numpy as jnp
import numpy as np
from jax import lax
from jax.experimental import pallas as pl
from jax.experimental.pallas import tpu as pltpu

D_MODEL = 1024
DEPTH = 4
GRID_W = 64
ROPE_THETA = 10000.0
CHUNK = 128
LN_EPS = 1e-5

A_HEADS = 4
A_QK_DIM = 32
A_V_DIM = 64
A_WIDTH = A_HEADS * A_V_DIM
B_HEADS = 8
B_KV_HEADS = 2
B_HEAD_DIM = 64
B_GROUP = B_HEADS // B_KV_HEADS
B_WIDTH = B_HEADS * B_HEAD_DIM
B_KV_COLS = B_KV_HEADS * B_HEAD_DIM
C_GROUPS = 4
C_GROUP_DIM = 64
C_WIDTH = C_GROUPS * C_GROUP_DIM
IN_COLS = 2048
FFN_HIDDEN = 2816
DEEPNORM_ALPHA = (2 * DEPTH) ** 0.25

LANES = 128
LOG2E = 1.4426950408889634
VMEM_LIMIT_BYTES = 56 * 1024 * 1024

OFF_AQ, OFF_AK, OFF_AV = 0, 256, 512
OFF_BQ, OFF_BK, OFF_BV = 768, 1280, 1408
OFF_CU, OFF_CV = 1536, 1792

TM_IN = 512
TQ = 512
TM_POST = 512
SUB_POST = 256

F32 = jnp.float32
BF16 = jnp.bfloat16


def _bq_permutation():
    n = np.arange(B_WIDTH)
    head = n // LANES + B_GROUP * ((n % LANES) // B_HEAD_DIM)
    return head * B_HEAD_DIM + n % B_HEAD_DIM


_BQ_PERM = _bq_permutation()


def _gelu_tanh(x):
    c = math.sqrt(2.0 / math.pi)
    return x * (0.5 * (1.0 + jnp.tanh(c * (x + 0.044715 * (x * x * x)))))


def _layer_norm(x, g, b):
    mu = jnp.mean(x, axis=-1, keepdims=True)
    xc = x - mu
    var = jnp.mean(xc * xc, axis=-1, keepdims=True)
    return xc * lax.rsqrt(var + LN_EPS) * g + b


def _rope128(xb, cos, sin_lo, sin_hi):
    return xb * cos + pltpu.roll(xb, LANES - 16, 1) * sin_lo + pltpu.roll(xb, 16, 1) * sin_hi


def _rms64(xb, lo, g128, post_scale):
    x2 = xb * xb
    s_lo = jnp.sum(jnp.where(lo, x2, 0.0), axis=-1, keepdims=True)
    s_hi = jnp.sum(jnp.where(lo, 0.0, x2), axis=-1, keepdims=True)
    r = lax.rsqrt(jnp.where(lo, s_lo, s_hi) * (1.0 / B_HEAD_DIM) + LN_EPS)
    return xb * r * g128 * post_scale


def _in_kernel(x_ref, w_ref, ta_ref, tb_ref, gq_ref, gk_ref, clg_ref, clb_ref, ws_ref, bsm_ref,
               aq_ref, akt_ref, av_ref, bq_ref, bkt_ref, bv_ref, co_ref, *, tm, blocks_per_seq):
    i = pl.program_id(0)
    t0 = pl.multiple_of((i % blocks_per_seq) * tm, tm)
    xb = x_ref[...].astype(BF16)

    def proj(off, width):
        return jnp.dot(xb, w_ref[:, off:off + width], preferred_element_type=F32)

    rows = pl.ds(t0, tm)
    a_cos, a_slo, a_shi = ta_ref[0, rows, :], ta_ref[1, rows, :], ta_ref[2, rows, :]
    b_cos, b_slo, b_shi = tb_ref[0, rows, :], tb_ref[1, rows, :], tb_ref[2, rows, :]
    lane = lax.broadcasted_iota(jnp.int32, (tm, LANES), 1)
    lo = lane < B_HEAD_DIM

    a_scale = A_QK_DIM ** -0.5 * LOG2E
    hq = proj(OFF_AQ, 256)
    hk = proj(OFF_AK, 256)
    for c in range(2):
        sl = slice(c * LANES, (c + 1) * LANES)
        aq_ref[:, sl] = (_rope128(hq[:, sl], a_cos, a_slo, a_shi) * a_scale).astype(BF16)
        akt_ref[sl, :] = _rope128(hk[:, sl], a_cos, a_slo, a_shi).T.astype(BF16)
    av_ref[...] = proj(OFF_AV, 256).astype(BF16)

    b_scale = B_HEAD_DIM ** -0.5 * LOG2E
    hq = proj(OFF_BQ, B_WIDTH)
    gq = gq_ref[...]
    for c in range(B_WIDTH // LANES):
        sl = slice(c * LANES, (c + 1) * LANES)
        qn = _rms64(hq[:, sl], lo, gq, b_scale)
        bq_ref[:, sl] = _rope128(qn, b_cos, b_slo, b_shi).astype(BF16)
    kn = _rms64(proj(OFF_BK, B_KV_COLS), lo, gk_ref[...], 1.0)
    bkt_ref[...] = _rope128(kn, b_cos, b_slo, b_shi).T.astype(BF16)
    bv_ref[...] = proj(OFF_BV, B_KV_COLS).astype(BF16)

    cu = _gelu_tanh(proj(OFF_CU, C_WIDTH))
    cv = _layer_norm(_gelu_tanh(proj(OFF_CV, C_WIDTH)), clg_ref[...], clb_ref[...]).astype(BF16)
    group = lax.broadcasted_iota(jnp.int32, (CHUNK, C_WIDTH), 1) // C_GROUP_DIM
    bsm = bsm_ref[...]
    for ci in range(tm // CHUNK):
        rs = slice(ci * CHUNK, (ci + 1) * CHUNK)
        vch = cv[rs, :]
        zero = jnp.zeros_like(vch)
        mixed = bsm
        for g in range(C_GROUPS):
            mixed = mixed + jnp.dot(ws_ref[g], jnp.where(group == g, vch, zero), preferred_element_type=F32)
        co_ref[rs, :] = (cu[rs, :] * mixed).astype(BF16)


def _in_call(x2d, w_in, ta, tb, gq, gk, clg, clb, ws, bsm, seq):
    m = x2d.shape[0]
    tm = TM_IN
    const2 = lambda i: (0, 0)
    const3 = lambda i: (0, 0, 0)
    row = lambda i: (i, 0)
    outs = ((256, False), (256, True), (256, False), (B_WIDTH, False), (B_KV_COLS, True),
            (B_KV_COLS, False), (C_WIDTH, False))
    col = lambda i: (0, i)
    return pl.pallas_call(
        functools.partial(_in_kernel, tm=tm, blocks_per_seq=seq // tm),
        grid=(m // tm,),
        in_specs=[
            pl.BlockSpec((tm, D_MODEL), row),
            pl.BlockSpec((D_MODEL, IN_COLS), const2),
            pl.BlockSpec((3, seq, LANES), const3),
            pl.BlockSpec((3, seq, LANES), const3),
            pl.BlockSpec((1, LANES), const2),
            pl.BlockSpec((1, LANES), const2),
            pl.BlockSpec((1, C_WIDTH), const2),
            pl.BlockSpec((1, C_WIDTH), const2),
            pl.BlockSpec((C_GROUPS, CHUNK, CHUNK), const3),
            pl.BlockSpec((CHUNK, C_WIDTH), const2),
        ],
        out_specs=[pl.BlockSpec((w, tm), col) if t else pl.BlockSpec((tm, w), row) for w, t in outs],
        out_shape=[jax.ShapeDtypeStruct((w, m) if t else (m, w), BF16) for w, t in outs],
        compiler_params=pltpu.CompilerParams(
            dimension_semantics=("arbitrary",), vmem_limit_bytes=VMEM_LIMIT_BYTES),
        name="in_proj",
    )(x2d, w_in, ta, tb, gq, gk, clg, clb, ws, bsm)


def _fill_v_ones(v_ref, v1_ref, col0):
    @pl.when(pl.program_id(1) == 0)
    def _():
        v1_ref[:, :LANES] = v_ref[:, col0:col0 + LANES]
        v1_ref[:, LANES:] = jnp.ones((v_ref.shape[0], LANES), BF16)


def _sweep_units(units, q_ref, kt_ref, s_refs, e_refs):
    def scores(u):
        qcol, krow, mask, _ = units[u]
        qc = q_ref[:, qcol * LANES:(qcol + 1) * LANES]
        lhs = jnp.where(mask, qc, jnp.zeros_like(qc))
        s_refs[u % 2][...] = jnp.dot(lhs, kt_ref[krow * LANES:(krow + 1) * LANES, :],
                                     preferred_element_type=F32)

    def probs(u):
        s = s_refs[u % 2][...]
        e_refs[u % 2][...] = jnp.exp2(s - jnp.max(s, axis=-1, keepdims=True)).astype(BF16)

    def values(u):
        o = jnp.dot(e_refs[u % 2][...], units[u][3][...], preferred_element_type=F32)
        return o[:, :LANES] / o[:, LANES:]

    outs = []
    scores(0)
    for u in range(len(units)):
        if u + 1 < len(units):
            scores(u + 1)
        probs(u)
        outs.append(values(u))
    return outs


def _attn_a_kernel(lq_ref, g_ref, q_ref, kt_ref, v_ref, o_ref, s0_ref, s1_ref, e0_ref, e1_ref,
                   va_ref, vb_ref, *, lam_init):
    lq = lq_ref[...]
    lam = (jnp.exp(jnp.sum(lq[0:1] * lq[1:2], axis=-1, keepdims=True))
           - jnp.exp(jnp.sum(lq[2:3] * lq[3:4], axis=-1, keepdims=True)) + lam_init)
    tq = q_ref.shape[0]
    lane = lax.broadcasted_iota(jnp.int32, (tq, LANES), 1)
    v1_refs = (va_ref, vb_ref)
    for cb in range(2):
        _fill_v_ones(v_ref, v1_refs[cb], cb * LANES)
    units = []
    for h in range(A_HEADS):
        for c in range(2):
            base = (h % 2) * A_V_DIM + c * A_QK_DIM
            units.append((h // 2, h // 2, (lane >= base) & (lane < base + A_QK_DIM), v1_refs[h // 2]))
    outs = _sweep_units(units, q_ref, kt_ref, (s0_ref, s1_ref), (e0_ref, e1_ref))
    lo = lane < A_V_DIM
    for cb in range(A_WIDTH // LANES):
        heads = [outs[2 * h] - lam * outs[2 * h + 1] for h in (2 * cb, 2 * cb + 1)]
        o = jnp.where(lo, heads[0], heads[1])
        o2 = o * o
        ms_lo = jnp.sum(jnp.where(lo, o2, 0.0), axis=-1, keepdims=True)
        ms_hi = jnp.sum(jnp.where(lo, 0.0, o2), axis=-1, keepdims=True)
        r = lax.rsqrt(jnp.where(lo, ms_lo, ms_hi) * (1.0 / A_V_DIM) + LN_EPS)
        o_ref[:, cb * LANES:(cb + 1) * LANES] = (o * r * g_ref[...] * (1.0 - lam_init)).astype(BF16)


def _attn_scratch(seq):
    return [pltpu.VMEM((TQ, seq), F32), pltpu.VMEM((TQ, seq), F32),
            pltpu.VMEM((TQ, seq), BF16), pltpu.VMEM((TQ, seq), BF16)]


def _attn_a_call(lq, g128, aq, akt, av, seq, lam_init):
    m = aq.shape[0]
    nq = seq // TQ
    const2 = lambda b, i: (0, 0)
    return pl.pallas_call(
        functools.partial(_attn_a_kernel, lam_init=lam_init),
        grid=(m // seq, nq),
        in_specs=[
            pl.BlockSpec((4, A_QK_DIM), const2),
            pl.BlockSpec((1, LANES), const2),
            pl.BlockSpec((TQ, A_WIDTH), lambda b, i: (b * nq + i, 0)),
            pl.BlockSpec((A_WIDTH, seq), lambda b, i: (0, b)),
            pl.BlockSpec((seq, A_WIDTH), lambda b, i: (b, 0)),
        ],
        out_specs=pl.BlockSpec((TQ, A_WIDTH), lambda b, i: (b * nq + i, 0)),
        out_shape=jax.ShapeDtypeStruct((m, A_WIDTH), BF16),
        scratch_shapes=_attn_scratch(seq) + [pltpu.VMEM((seq, 2 * LANES), BF16)] * 2,
        compiler_params=pltpu.CompilerParams(
            dimension_semantics=("arbitrary", "arbitrary"), vmem_limit_bytes=VMEM_LIMIT_BYTES),
        name="attn_a",
    )(lq, g128, aq, akt, av)


def _attn_b_kernel(q_ref, kt_ref, v_ref, o_ref, s0_ref, s1_ref, e0_ref, e1_ref, v1_ref):
    tq = q_ref.shape[0]
    lo = lax.broadcasted_iota(jnp.int32, (tq, LANES), 1) < B_HEAD_DIM
    _fill_v_ones(v_ref, v1_ref, 0)
    units = [(c, 0, lo if half == 0 else ~lo, v1_ref) for c in range(B_WIDTH // LANES) for half in range(2)]
    outs = _sweep_units(units, q_ref, kt_ref, (s0_ref, s1_ref), (e0_ref, e1_ref))
    for c in range(B_WIDTH // LANES):
        o_ref[:, c * LANES:(c + 1) * LANES] = jnp.where(lo, outs[2 * c], outs[2 * c + 1]).astype(BF16)


def _attn_b_call(bq, bkt, bv, seq):
    m = bq.shape[0]
    nq = seq // TQ
    return pl.pallas_call(
        _attn_b_kernel,
        grid=(m // seq, nq),
        in_specs=[
            pl.BlockSpec((TQ, B_WIDTH), lambda b, i: (b * nq + i, 0)),
            pl.BlockSpec((B_KV_COLS, seq), lambda b, i: (0, b)),
            pl.BlockSpec((seq, B_KV_COLS), lambda b, i: (b, 0)),
        ],
        out_specs=pl.BlockSpec((TQ, B_WIDTH), lambda b, i: (b * nq + i, 0)),
        out_shape=jax.ShapeDtypeStruct((m, B_WIDTH), BF16),
        scratch_shapes=_attn_scratch(seq) + [pltpu.VMEM((seq, 2 * LANES), BF16)],
        compiler_params=pltpu.CompilerParams(
            dimension_semantics=("arbitrary", "arbitrary"), vmem_limit_bytes=VMEM_LIMIT_BYTES),
        name="attn_b",
    )(bq, bkt, bv)


def _post_kernel(x_ref, a_ref, b_ref, c_ref, wa_ref, wb_ref, wc_ref, g1_ref, b1_ref,
                 wg_ref, wu_ref, wd_ref, g2_ref, b2_ref, o_ref):
    tm = x_ref.shape[0]
    n_sub = tm // SUB_POST
    rows = [slice(j * SUB_POST, (j + 1) * SUB_POST) for j in range(n_sub)]

    def mix(r):
        return (jnp.dot(a_ref[r, :], wa_ref[...], preferred_element_type=F32)
                + jnp.dot(b_ref[r, :], wb_ref[...], preferred_element_type=F32)
                + jnp.dot(c_ref[r, :], wc_ref[...], preferred_element_type=F32))

    def ln1(r, m):
        return _layer_norm(DEEPNORM_ALPHA * x_ref[r, :] + m, g1_ref[...], b1_ref[...])

    def hidden(y):
        yb = y.astype(BF16)
        gate = jnp.dot(yb, wg_ref[...], preferred_element_type=F32)
        up = jnp.dot(yb, wu_ref[...], preferred_element_type=F32)
        return (gate * jax.nn.sigmoid(gate) * up).astype(BF16)

    def down(hid):
        return jnp.dot(hid, wd_ref[...], preferred_element_type=F32)

    def ln2(r, y, f):
        o_ref[r, :] = _layer_norm(DEEPNORM_ALPHA * y + f, g2_ref[...], b2_ref[...])

    mixes = [mix(r) for r in rows]
    ys = [None] * n_sub
    ys[0] = ln1(rows[0], mixes[0])
    ffn_prev = None
    for j in range(n_sub):
        hid = hidden(ys[j])
        if j + 1 < n_sub:
            ys[j + 1] = ln1(rows[j + 1], mixes[j + 1])
        if j > 0:
            ln2(rows[j - 1], ys[j - 1], ffn_prev)
        ffn_prev = down(hid)
    ln2(rows[-1], ys[-1], ffn_prev)


def _post_call(x2d, ao, bo, co, wa, wb, wc, g1, b1, wg, wu, wd, g2, b2):
    m = x2d.shape[0]
    tm = TM_POST
    row = lambda i: (i, 0)
    const2 = lambda i: (0, 0)

    def resident(shape):
        return pl.BlockSpec(shape, const2, pipeline_mode=pl.Buffered(1))

    return pl.pallas_call(
        _post_kernel,
        grid=(m // tm,),
        in_specs=[
            pl.BlockSpec((tm, D_MODEL), row),
            pl.BlockSpec((tm, A_WIDTH), row),
            pl.BlockSpec((tm, B_WIDTH), row),
            pl.BlockSpec((tm, C_WIDTH), row),
            resident((A_WIDTH, D_MODEL)),
            resident((B_WIDTH, D_MODEL)),
            resident((C_WIDTH, D_MODEL)),
            resident((1, D_MODEL)),
            resident((1, D_MODEL)),
            resident((D_MODEL, FFN_HIDDEN)),
            resident((D_MODEL, FFN_HIDDEN)),
            resident((FFN_HIDDEN, D_MODEL)),
            resident((1, D_MODEL)),
            resident((1, D_MODEL)),
        ],
        out_specs=pl.BlockSpec((tm, D_MODEL), row),
        out_shape=jax.ShapeDtypeStruct((m, D_MODEL), F32),
        compiler_params=pltpu.CompilerParams(
            dimension_semantics=("arbitrary",), vmem_limit_bytes=VMEM_LIMIT_BYTES),
        name="post",
    )(x2d, ao, bo, co, wa, wb, wc, g1, b1, wg, wu, wd, g2, b2)


def _rope_tables(pos, dim):
    inv = 1.0 / (ROPE_THETA ** (jnp.arange(0, dim, 2, dtype=F32) / dim))
    ang = pos.astype(F32)[:, None] * inv[None, :]
    ang = jnp.concatenate([ang, ang], axis=-1)
    return jnp.cos(ang), jnp.sin(ang)


def _signed_tables(cos, sin):
    first = (jnp.arange(LANES) % 32) < 16
    return jnp.stack([cos, jnp.where(first, -sin, 0.0), jnp.where(first, 0.0, sin)])


def kernel(x, w_in, w_out, lam_qk, a_subln_g, b_q_norm_g, b_k_norm_g, c_ln_g, c_ln_b, c_w_s, c_b_s,
           ln1_g, ln1_b, w_gate, w_up, w_down, ln2_g, ln2_b):
    bsz, seq, _ = x.shape
    assert seq % TM_IN == 0 and seq % TQ == 0 and seq % GRID_W == 0 and (bsz * seq) % TM_POST == 0
    t = jnp.arange(seq, dtype=jnp.int32)
    cos_a, sin_a = _rope_tables(t, A_QK_DIM)
    cos_r, sin_r = _rope_tables(t // GRID_W, B_HEAD_DIM // 2)
    cos_c, sin_c = _rope_tables(t % GRID_W, B_HEAD_DIM // 2)
    ta = _signed_tables(jnp.tile(cos_a, (1, 4)), jnp.tile(sin_a, (1, 4)))
    tb = _signed_tables(jnp.tile(jnp.concatenate([cos_r, cos_c], -1), (1, 2)),
                        jnp.tile(jnp.concatenate([sin_r, sin_c], -1), (1, 2)))

    in_perm = np.arange(IN_COLS)
    in_perm[OFF_BQ:OFF_BQ + B_WIDTH] = OFF_BQ + _BQ_PERM
    out_perm = np.arange(D_MODEL)
    out_perm[A_WIDTH:A_WIDTH + B_WIDTH] = A_WIDTH + _BQ_PERM
    w_in_b = w_in[:, :, in_perm].astype(BF16)
    w_out_b = w_out[:, out_perm, :].astype(BF16)
    w_gate_b, w_up_b, w_down_b = w_gate.astype(BF16), w_up.astype(BF16), w_down.astype(BF16)
    w_s_b = c_w_s.astype(BF16)

    x2d = x.reshape(bsz * seq, D_MODEL)
    for l in range(DEPTH):
        lam_init = 0.8 - 0.6 * math.exp(-0.3 * l)
        wa = w_out_b[l, :A_WIDTH]
        wb = w_out_b[l, A_WIDTH:A_WIDTH + B_WIDTH]
        wc = w_out_b[l, A_WIDTH + B_WIDTH:]
        bsm = jnp.repeat(c_b_s[l].T, C_GROUP_DIM, axis=1)
        aq, akt, av, bq, bkt, bv, co = _in_call(
            x2d, w_in_b[l], ta, tb,
            jnp.tile(b_q_norm_g[l], 2)[None], jnp.tile(b_k_norm_g[l], 2)[None],
            c_ln_g[l][None], c_ln_b[l][None], w_s_b[l], bsm, seq)
        ao = _attn_a_call(lam_qk[l], jnp.tile(a_subln_g[l], 2)[None], aq, akt, av, seq, lam_init)
        bo = _attn_b_call(bq, bkt, bv, seq)
        x2d = _post_call(x2d, ao, bo, co, wa, wb, wc, ln1_g[l][None], ln1_b[l][None],
                         w_gate_b[l], w_up_b[l], w_down_b[l], ln2_g[l][None], ln2_b[l][None])
    return x2d.reshape(bsz, seq, D_MODEL)
```

```python
import functools
import math

import jax
import jax.numpy as jnp
import numpy as np
from jax import lax
from jax.experimental import pallas as pl
from jax.experimental.pallas import tpu as pltpu

D_MODEL = 1024
DEPTH = 4
GRID_W = 64
ROPE_THETA = 10000.0
CHUNK = 128
LN_EPS = 1e-5

A_HEADS = 4
A_QK_DIM = 32
A_V_DIM = 64
A_WIDTH = A_HEADS * A_V_DIM
B_HEADS = 8
B_KV_HEADS = 2
B_HEAD_DIM = 64
B_GROUP = B_HEADS // B_KV_HEADS
B_WIDTH = B_HEADS * B_HEAD_DIM
B_KV_COLS = B_KV_HEADS * B_HEAD_DIM
C_GROUPS = 4
C_GROUP_DIM = 64
C_WIDTH = C_GROUPS * C_GROUP_DIM
IN_COLS = 2048
FFN_HIDDEN = 2816
DEEPNORM_ALPHA = (2 * DEPTH) ** 0.25

LANES = 128
LOG2E = 1.4426950408889634
VMEM_LIMIT_BYTES = 56 * 1024 * 1024

OFF_AQ, OFF_AK, OFF_AV = 0, 256, 512
OFF_BQ, OFF_BK, OFF_BV = 768, 1280, 1408
OFF_CU, OFF_CV = 1536, 1792

TM_IN = 1024
SUB_IN = 256
TQ = 1024
TM_POST = 512
SUB_POST = 256

F32 = jnp.float32
BF16 = jnp.bfloat16


def _bq_permutation():
    n = np.arange(B_WIDTH)
    head = n // LANES + B_GROUP * ((n % LANES) // B_HEAD_DIM)
    return head * B_HEAD_DIM + n % B_HEAD_DIM


_BQ_PERM = _bq_permutation()


def _gelu_tanh(x):
    c = math.sqrt(2.0 / math.pi)
    return x * (0.5 * (1.0 + jnp.tanh(c * (x + 0.044715 * (x * x * x)))))


def _layer_norm(x, g, b):
    mu = jnp.mean(x, axis=-1, keepdims=True)
    xc = x - mu
    var = jnp.mean(xc * xc, axis=-1, keepdims=True)
    return xc * lax.rsqrt(var + LN_EPS) * g + b


def _rope128(xb, cos, sin_signed):
    return xb * cos + pltpu.roll(xb, LANES // 2, 1) * sin_signed


def _rms64(xb, lo, g128, post_scale):
    x2 = xb * xb
    s_lo = jnp.sum(jnp.where(lo, x2, 0.0), axis=-1, keepdims=True)
    s_hi = jnp.sum(jnp.where(lo, 0.0, x2), axis=-1, keepdims=True)
    r = lax.rsqrt(jnp.where(lo, s_lo, s_hi) * (1.0 / B_HEAD_DIM) + LN_EPS)
    return xb * r * g128 * post_scale


def _in_kernel(x_ref, w_ref, ta_ref, tb_ref, gq_ref, gk_ref, clg_ref, clb_ref, ws_ref, bsm_ref,
               aq_ref, akt_ref, av_ref, bq_ref, bkt_ref, bv_ref, co_ref, h0_ref, h1_ref,
               *, tm, blocks_per_seq):
    i = pl.program_id(0)
    t0 = (i % blocks_per_seq) * tm
    lane = lax.broadcasted_iota(jnp.int32, (SUB_IN, LANES), 1)
    lo = (lane % B_HEAD_DIM) < B_HEAD_DIM // 2
    group = lax.broadcasted_iota(jnp.int32, (CHUNK, C_WIDTH), 1) // C_GROUP_DIM
    a_scale = A_QK_DIM ** -0.5 * LOG2E
    b_scale = B_HEAD_DIM ** -0.5 * LOG2E
    h_refs = (h0_ref, h1_ref)

    def project(j):
        xb = x_ref[j * SUB_IN:(j + 1) * SUB_IN, :].astype(BF16)
        h_refs[j % 2][...] = jnp.dot(xb, w_ref[...], preferred_element_type=F32)

    def epilogue(j):
        h_ref = h_refs[j % 2]
        r = slice(j * SUB_IN, (j + 1) * SUB_IN)
        rows = pl.ds(pl.multiple_of(t0 + j * SUB_IN, SUB_IN), SUB_IN)

        cos, sin = ta_ref[0, rows, :], ta_ref[1, rows, :]
        for c in range(2):
            sl = slice(c * LANES, (c + 1) * LANES)
            hq = h_ref[:, OFF_AQ + c * LANES:OFF_AQ + (c + 1) * LANES]
            hk = h_ref[:, OFF_AK + c * LANES:OFF_AK + (c + 1) * LANES]
            aq_ref[r, sl] = (_rope128(hq, cos, sin) * a_scale).astype(BF16)
            akt_ref[sl, r] = _rope128(hk, cos, sin).T.astype(BF16)
        av_ref[r, :] = h_ref[:, OFF_AV:OFF_AV + 256].astype(BF16)

        cos, sin = tb_ref[0, rows, :], tb_ref[1, rows, :]
        gq = gq_ref[...]
        for c in range(B_WIDTH // LANES):
            sl = slice(c * LANES, (c + 1) * LANES)
            qn = _rms64(h_ref[:, OFF_BQ + c * LANES:OFF_BQ + (c + 1) * LANES], lo, gq, b_scale)
            bq_ref[r, sl] = _rope128(qn, cos, sin).astype(BF16)
        kn = _rms64(h_ref[:, OFF_BK:OFF_BK + B_KV_COLS], lo, gk_ref[...], 1.0)
        bkt_ref[:, r] = _rope128(kn, cos, sin).T.astype(BF16)
        bv_ref[r, :] = h_ref[:, OFF_BV:OFF_BV + B_KV_COLS].astype(BF16)

        cu = _gelu_tanh(h_ref[:, OFF_CU:OFF_CU + C_WIDTH])
        cv = _layer_norm(_gelu_tanh(h_ref[:, OFF_CV:OFF_CV + C_WIDTH]), clg_ref[...], clb_ref[...]).astype(BF16)
        for ci in range(SUB_IN // CHUNK):
            rs = slice(ci * CHUNK, (ci + 1) * CHUNK)
            vch = cv[rs, :]
            zero = jnp.zeros_like(vch)
            mixed = bsm_ref[...]
            for g in range(C_GROUPS):
                mixed = mixed + jnp.dot(ws_ref[g], jnp.where(group == g, vch, zero),
                                        preferred_element_type=F32)
            co_ref[j * SUB_IN + ci * CHUNK:j * SUB_IN + (ci + 1) * CHUNK, :] = (cu[rs, :] * mixed).astype(BF16)

    n_sub = tm // SUB_IN
    project(0)
    for j in range(n_sub):
        if j + 1 < n_sub:
            project(j + 1)
        epilogue(j)


def _in_call(x2d, w_in, ta, tb, gq, gk, clg, clb, ws, bsm, seq):
    m = x2d.shape[0]
    tm = TM_IN
    const2 = lambda i: (0, 0)
    const3 = lambda i: (0, 0, 0)
    row = lambda i: (i, 0)
    outs = ((256, False), (256, True), (256, False), (B_WIDTH, False), (B_KV_COLS, True),
            (B_KV_COLS, False), (C_WIDTH, False))
    col = lambda i: (0, i)
    return pl.pallas_call(
        functools.partial(_in_kernel, tm=tm, blocks_per_seq=seq // tm),
        grid=(m // tm,),
        in_specs=[
            pl.BlockSpec((tm, D_MODEL), row),
            pl.BlockSpec((D_MODEL, IN_COLS), const2, pipeline_mode=pl.Buffered(1)),
            pl.BlockSpec((2, seq, LANES), const3, pipeline_mode=pl.Buffered(1)),
            pl.BlockSpec((2, seq, LANES), const3, pipeline_mode=pl.Buffered(1)),
            pl.BlockSpec((1, LANES), const2),
            pl.BlockSpec((1, LANES), const2),
            pl.BlockSpec((1, C_WIDTH), const2),
            pl.BlockSpec((1, C_WIDTH), const2),
            pl.BlockSpec((C_GROUPS, CHUNK, CHUNK), const3),
            pl.BlockSpec((CHUNK, C_WIDTH), const2),
        ],
        out_specs=[pl.BlockSpec((w, tm), col) if t else pl.BlockSpec((tm, w), row) for w, t in outs],
        out_shape=[jax.ShapeDtypeStruct((w, m) if t else (m, w), BF16) for w, t in outs],
        scratch_shapes=[pltpu.VMEM((SUB_IN, IN_COLS), F32)] * 2,
        compiler_params=pltpu.CompilerParams(
            dimension_semantics=("arbitrary",), vmem_limit_bytes=VMEM_LIMIT_BYTES),
        name="in_proj",
    )(x2d, w_in, ta, tb, gq, gk, clg, clb, ws, bsm)


def _fill_v_ones(v_ref, v1_ref, col0):
    @pl.when(pl.program_id(1) == 0)
    def _():
        v1_ref[:, :LANES] = v_ref[:, col0:col0 + LANES]
        v1_ref[:, LANES:] = jnp.ones((v_ref.shape[0], LANES), BF16)


def _sweep_units(units, q_ref, kt_ref, s_refs, e_refs):
    def scores(u):
        qcol, krow, mask, _ = units[u]
        qc = q_ref[:, qcol * LANES:(qcol + 1) * LANES]
        lhs = jnp.where(mask, qc, jnp.zeros_like(qc))
        s_refs[u % 2][...] = jnp.dot(lhs, kt_ref[krow * LANES:(krow + 1) * LANES, :],
                                     preferred_element_type=F32)

    def probs(u):
        s = s_refs[u % 2][...]
        e_refs[u % 2][...] = jnp.exp2(s - jnp.max(s, axis=-1, keepdims=True)).astype(BF16)

    def values(u):
        o = jnp.dot(e_refs[u % 2][...], units[u][3][...], preferred_element_type=F32)
        return o[:, :LANES] / o[:, LANES:]

    outs = []
    scores(0)
    for u in range(len(units)):
        if u + 1 < len(units):
            scores(u + 1)
        probs(u)
        outs.append(values(u))
    return outs


def _attn_a_kernel(lq_ref, g_ref, q_ref, kt_ref, v_ref, o_ref, s0_ref, s1_ref, e0_ref, e1_ref,
                   va_ref, vb_ref, *, lam_init):
    lq = lq_ref[...]
    lam = (jnp.exp(jnp.sum(lq[0:1] * lq[1:2], axis=-1, keepdims=True))
           - jnp.exp(jnp.sum(lq[2:3] * lq[3:4], axis=-1, keepdims=True)) + lam_init)
    tq = q_ref.shape[0]
    lane = lax.broadcasted_iota(jnp.int32, (tq, LANES), 1)
    v1_refs = (va_ref, vb_ref)
    for cb in range(2):
        _fill_v_ones(v_ref, v1_refs[cb], cb * LANES)
    units = []
    for h in range(A_HEADS):
        for c in range(2):
            group = 2 * (h % 2) + c
            units.append((h // 2, h // 2, (lane % 64) // 16 == group, v1_refs[h // 2]))
    outs = _sweep_units(units, q_ref, kt_ref, (s0_ref, s1_ref), (e0_ref, e1_ref))
    lo = lane < A_V_DIM
    for cb in range(A_WIDTH // LANES):
        heads = [outs[2 * h] - lam * outs[2 * h + 1] for h in (2 * cb, 2 * cb + 1)]
        o = jnp.where(lo, heads[0], heads[1])
        o2 = o * o
        ms_lo = jnp.sum(jnp.where(lo, o2, 0.0), axis=-1, keepdims=True)
        ms_hi = jnp.sum(jnp.where(lo, 0.0, o2), axis=-1, keepdims=True)
        r = lax.rsqrt(jnp.where(lo, ms_lo, ms_hi) * (1.0 / A_V_DIM) + LN_EPS)
        o_ref[:, cb * LANES:(cb + 1) * LANES] = (o * r * g_ref[...] * (1.0 - lam_init)).astype(BF16)


def _attn_scratch(seq):
    return [pltpu.VMEM((TQ, seq), F32), pltpu.VMEM((TQ, seq), F32),
            pltpu.VMEM((TQ, seq), BF16), pltpu.VMEM((TQ, seq), BF16)]


def _attn_a_call(lq, g128, aq, akt, av, seq, lam_init):
    m = aq.shape[0]
    nq = seq // TQ
    const2 = lambda b, i: (0, 0)
    return pl.pallas_call(
        functools.partial(_attn_a_kernel, lam_init=lam_init),
        grid=(m // seq, nq),
        in_specs=[
            pl.BlockSpec((4, A_QK_DIM), const2),
            pl.BlockSpec((1, LANES), const2),
            pl.BlockSpec((TQ, A_WIDTH), lambda b, i: (b * nq + i, 0)),
            pl.BlockSpec((A_WIDTH, seq), lambda b, i: (0, b)),
            pl.BlockSpec((seq, A_WIDTH), lambda b, i: (b, 0)),
        ],
        out_specs=pl.BlockSpec((TQ, A_WIDTH), lambda b, i: (b * nq + i, 0)),
        out_shape=jax.ShapeDtypeStruct((m, A_WIDTH), BF16),
        scratch_shapes=_attn_scratch(seq) + [pltpu.VMEM((seq, 2 * LANES), BF16)] * 2,
        compiler_params=pltpu.CompilerParams(
            dimension_semantics=("arbitrary", "arbitrary"), vmem_limit_bytes=VMEM_LIMIT_BYTES),
        name="attn_a",
    )(lq, g128, aq, akt, av)


def _attn_b_kernel(q_ref, kt_ref, v_ref, o_ref, s0_ref, s1_ref, e0_ref, e1_ref, v1_ref):
    tq = q_ref.shape[0]
    lane = lax.broadcasted_iota(jnp.int32, (tq, LANES), 1)
    lo = lane < B_HEAD_DIM
    first = (lane % B_HEAD_DIM) < B_HEAD_DIM // 2
    _fill_v_ones(v_ref, v1_ref, 0)
    units = [(c, 0, first if half == 0 else ~first, v1_ref)
             for c in range(B_WIDTH // LANES) for half in range(2)]
    outs = _sweep_units(units, q_ref, kt_ref, (s0_ref, s1_ref), (e0_ref, e1_ref))
    for c in range(B_WIDTH // LANES):
        o_ref[:, c * LANES:(c + 1) * LANES] = jnp.where(lo, outs[2 * c], outs[2 * c + 1]).astype(BF16)


def _attn_b_call(bq, bkt, bv, seq):
    m = bq.shape[0]
    nq = seq // TQ
    return pl.pallas_call(
        _attn_b_kernel,
        grid=(m // seq, nq),
        in_specs=[
            pl.BlockSpec((TQ, B_WIDTH), lambda b, i: (b * nq + i, 0)),
            pl.BlockSpec((B_KV_COLS, seq), lambda b, i: (0, b)),
            pl.BlockSpec((seq, B_KV_COLS), lambda b, i: (b, 0)),
        ],
        out_specs=pl.BlockSpec((TQ, B_WIDTH), lambda b, i: (b * nq + i, 0)),
        out_shape=jax.ShapeDtypeStruct((m, B_WIDTH), BF16),
        scratch_shapes=_attn_scratch(seq) + [pltpu.VMEM((seq, 2 * LANES), BF16)],
        compiler_params=pltpu.CompilerParams(
            dimension_semantics=("arbitrary", "arbitrary"), vmem_limit_bytes=VMEM_LIMIT_BYTES),
        name="attn_b",
    )(bq, bkt, bv)


def _post_kernel(x_ref, a_ref, b_ref, c_ref, wa_ref, wb_ref, wc_ref, g1_ref, b1_ref,
                 wg_ref, wu_ref, wd_ref, g2_ref, b2_ref, o_ref):
    tm = x_ref.shape[0]
    n_sub = tm // SUB_POST
    rows = [slice(j * SUB_POST, (j + 1) * SUB_POST) for j in range(n_sub)]

    def mix(r):
        return (jnp.dot(a_ref[r, :], wa_ref[...], preferred_element_type=F32)
                + jnp.dot(b_ref[r, :], wb_ref[...], preferred_element_type=F32)
                + jnp.dot(c_ref[r, :], wc_ref[...], preferred_element_type=F32))

    def ln1(r, m):
        return _layer_norm(DEEPNORM_ALPHA * x_ref[r, :] + m, g1_ref[...], b1_ref[...])

    def hidden(y):
        yb = y.astype(BF16)
        gate = jnp.dot(yb, wg_ref[...], preferred_element_type=F32)
        up = jnp.dot(yb, wu_ref[...], preferred_element_type=F32)
        return (gate * jax.nn.sigmoid(gate) * up).astype(BF16)

    def down(hid):
        return jnp.dot(hid, wd_ref[...], preferred_element_type=F32)

    def ln2(r, y, f):
        o_ref[r, :] = _layer_norm(DEEPNORM_ALPHA * y + f, g2_ref[...], b2_ref[...])

    mixes = [mix(r) for r in rows]
    ys = [None] * n_sub
    ys[0] = ln1(rows[0], mixes[0])
    ffn_prev = None
    for j in range(n_sub):
        hid = hidden(ys[j])
        if j + 1 < n_sub:
            ys[j + 1] = ln1(rows[j + 1], mixes[j + 1])
        if j > 0:
            ln2(rows[j - 1], ys[j - 1], ffn_prev)
        ffn_prev = down(hid)
    ln2(rows[-1], ys[-1], ffn_prev)


def _post_call(x2d, ao, bo, co, wa, wb, wc, g1, b1, wg, wu, wd, g2, b2):
    m = x2d.shape[0]
    tm = TM_POST
    row = lambda i: (i, 0)
    const2 = lambda i: (0, 0)

    def resident(shape):
        return pl.BlockSpec(shape, const2, pipeline_mode=pl.Buffered(1))

    return pl.pallas_call(
        _post_kernel,
        grid=(m // tm,),
        in_specs=[
            pl.BlockSpec((tm, D_MODEL), row),
            pl.BlockSpec((tm, A_WIDTH), row),
            pl.BlockSpec((tm, B_WIDTH), row),
            pl.BlockSpec((tm, C_WIDTH), row),
            resident((A_WIDTH, D_MODEL)),
            resident((B_WIDTH, D_MODEL)),
            resident((C_WIDTH, D_MODEL)),
            resident((1, D_MODEL)),
            resident((1, D_MODEL)),
            resident((D_MODEL, FFN_HIDDEN)),
            resident((D_MODEL, FFN_HIDDEN)),
            resident((FFN_HIDDEN, D_MODEL)),
            resident((1, D_MODEL)),
            resident((1, D_MODEL)),
        ],
        out_specs=pl.BlockSpec((tm, D_MODEL), row),
        out_shape=jax.ShapeDtypeStruct((m, D_MODEL), F32),
        compiler_params=pltpu.CompilerParams(
            dimension_semantics=("arbitrary",), vmem_limit_bytes=VMEM_LIMIT_BYTES),
        name="post",
    )(x2d, ao, bo, co, wa, wb, wc, g1, b1, wg, wu, wd, g2, b2)


def _rope_tables(pos, dim):
    inv = 1.0 / (ROPE_THETA ** (jnp.arange(0, dim, 2, dtype=F32) / dim))
    ang = pos.astype(F32)[:, None] * inv[None, :]
    ang = jnp.concatenate([ang, ang], axis=-1)
    return jnp.cos(ang), jnp.sin(ang)


def _lane_dims():
    lane = np.arange(LANES)
    half, g, j = lane // 64, (lane % 64) // 16, lane % 16
    return g, 16 * half + j, half


def _rope_block_tables(cos_groups, sin_groups):
    g, d, half = _lane_dims()
    cos = jnp.stack([cos_groups[gi][:, di] for gi, di in zip(g, d)], axis=1)
    sin = jnp.stack([sin_groups[gi][:, di] for gi, di in zip(g, d)], axis=1)
    return jnp.stack([cos, jnp.where(half == 0, -sin, sin)])


def kernel(x, w_in, w_out, lam_qk, a_subln_g, b_q_norm_g, b_k_norm_g, c_ln_g, c_ln_b, c_w_s, c_b_s,
           ln1_g, ln1_b, w_gate, w_up, w_down, ln2_g, ln2_b):
    bsz, seq, _ = x.shape
    assert seq % TM_IN == 0 and seq % TQ == 0 and seq % GRID_W == 0 and (bsz * seq) % TM_POST == 0
    t = jnp.arange(seq, dtype=jnp.int32)
    cos_a, sin_a = _rope_tables(t, A_QK_DIM)
    cos_r, sin_r = _rope_tables(t // GRID_W, B_HEAD_DIM // 2)
    cos_c, sin_c = _rope_tables(t % GRID_W, B_HEAD_DIM // 2)
    ta = _rope_block_tables([cos_a] * 4, [sin_a] * 4)
    tb = _rope_block_tables([cos_r, cos_c] * 2, [sin_r, sin_c] * 2)

    g, d, _ = _lane_dims()
    block_perm = 32 * g + d
    in_perm = np.arange(IN_COLS)
    for off, width in ((OFF_AQ, 256), (OFF_AK, 256), (OFF_BK, B_KV_COLS)):
        for c in range(width // LANES):
            in_perm[off + c * LANES:off + (c + 1) * LANES] = off + c * LANES + block_perm
    for c in range(B_WIDTH // LANES):
        in_perm[OFF_BQ + c * LANES:OFF_BQ + (c + 1) * LANES] = OFF_BQ + _BQ_PERM[c * LANES + block_perm]
    out_perm = np.arange(D_MODEL)
    out_perm[A_WIDTH:A_WIDTH + B_WIDTH] = A_WIDTH + _BQ_PERM
    gain_perm = block_perm % B_HEAD_DIM
    w_in_b = w_in[:, :, in_perm].astype(BF16)
    w_out_b = w_out[:, out_perm, :].astype(BF16)
    w_gate_b, w_up_b, w_down_b = w_gate.astype(BF16), w_up.astype(BF16), w_down.astype(BF16)
    w_s_b = c_w_s.astype(BF16)

    x2d = x.reshape(bsz * seq, D_MODEL)
    for l in range(DEPTH):
        lam_init = 0.8 - 0.6 * math.exp(-0.3 * l)
        wa = w_out_b[l, :A_WIDTH]
        wb = w_out_b[l, A_WIDTH:A_WIDTH + B_WIDTH]
        wc = w_out_b[l, A_WIDTH + B_WIDTH:]
        bsm = jnp.repeat(c_b_s[l].T, C_GROUP_DIM, axis=1)
        aq, akt, av, bq, bkt, bv, co = _in_call(
            x2d, w_in_b[l], ta, tb,
            b_q_norm_g[l][gain_perm][None], b_k_norm_g[l][gain_perm][None],
            c_ln_g[l][None], c_ln_b[l][None], w_s_b[l], bsm, seq)
        ao = _attn_a_call(lam_qk[l], jnp.tile(a_subln_g[l], 2)[None], aq, akt, av, seq, lam_init)
        bo = _attn_b_call(bq, bkt, bv, seq)
        x2d = _post_call(x2d, ao, bo, co, wa, wb, wc, ln1_g[l][None], ln1_b[l][None],
                         w_gate_b[l], w_up_b[l], w_down_b[l], ln2_g[l][None], ln2_b[l][None])
    return x2d.reshape(bsz, seq, D_MODEL)
```

```python
import functools
import math

import jax
import jax.numpy as jnp
import numpy as np
from jax import lax
from jax.experimental import pallas as pl
from jax.experimental.pallas import tpu as pltpu

D_MODEL = 1024
DEPTH = 4
GRID_W = 64
ROPE_THETA = 10000.0
CHUNK = 128
LN_EPS = 1e-5

A_HEADS = 4
A_QK_DIM = 32
A_V_DIM = 64
A_WIDTH = A_HEADS * A_V_DIM
B_HEADS = 8
B_KV_HEADS = 2
B_HEAD_DIM = 64
B_GROUP = B_HEADS // B_KV_HEADS
B_WIDTH = B_HEADS * B_HEAD_DIM
B_KV_COLS = B_KV_HEADS * B_HEAD_DIM
C_GROUPS = 4
C_GROUP_DIM = 64
C_WIDTH = C_GROUPS * C_GROUP_DIM
IN_COLS = 2048
FFN_HIDDEN = 2816
DEEPNORM_ALPHA = (2 * DEPTH) ** 0.25

LANES = 128
LOG2E = 1.4426950408889634
VMEM_LIMIT_BYTES = 56 * 1024 * 1024

OFF_AQ, OFF_AK, OFF_AV = 0, 256, 512
OFF_BQ, OFF_BK, OFF_BV = 768, 1280, 1408
OFF_CU, OFF_CV = 1536, 1792

TM_IN = 1024
SUB_IN = 256
TQ = 1024
TM_POST = 512
SUB_POST = 256

F32 = jnp.float32
BF16 = jnp.bfloat16


def _bq_permutation():
    n = np.arange(B_WIDTH)
    head = n // LANES + B_GROUP * ((n % LANES) // B_HEAD_DIM)
    return head * B_HEAD_DIM + n % B_HEAD_DIM


_BQ_PERM = _bq_permutation()


def _gelu_tanh(x):
    c = math.sqrt(2.0 / math.pi)
    return x * (0.5 * (1.0 + jnp.tanh(c * (x + 0.044715 * (x * x * x)))))


def _layer_norm(x, g, b):
    mu = jnp.mean(x, axis=-1, keepdims=True)
    xc = x - mu
    var = jnp.mean(xc * xc, axis=-1, keepdims=True)
    return xc * lax.rsqrt(var + LN_EPS) * g + b


def _rope128(xb, cos, sin_signed):
    return xb * cos + pltpu.roll(xb, LANES // 2, 1) * sin_signed


def _rms64(xb, lo, g128, post_scale):
    x2 = xb * xb
    s_lo = jnp.sum(jnp.where(lo, x2, 0.0), axis=-1, keepdims=True)
    s_hi = jnp.sum(jnp.where(lo, 0.0, x2), axis=-1, keepdims=True)
    r = lax.rsqrt(jnp.where(lo, s_lo, s_hi) * (1.0 / B_HEAD_DIM) + LN_EPS)
    return xb * r * g128 * post_scale


def _in_kernel(x_ref, w_ref, ta_ref, tb_ref, gq_ref, gk_ref, clg_ref, clb_ref, ws_ref, bsm_ref,
               aq_ref, akt_ref, av_ref, bq_ref, bkt_ref, bv_ref, co_ref, h0_ref, h1_ref,
               *, tm, blocks_per_seq):
    i = pl.program_id(0)
    t0 = (i % blocks_per_seq) * tm
    lane = lax.broadcasted_iota(jnp.int32, (SUB_IN, LANES), 1)
    lo = (lane % B_HEAD_DIM) < B_HEAD_DIM // 2
    group = lax.broadcasted_iota(jnp.int32, (CHUNK, C_WIDTH), 1) // C_GROUP_DIM
    a_scale = A_QK_DIM ** -0.5 * LOG2E
    b_scale = B_HEAD_DIM ** -0.5 * LOG2E
    h_refs = (h0_ref, h1_ref)

    def project(j):
        xb = x_ref[j * SUB_IN:(j + 1) * SUB_IN, :].astype(BF16)
        h_refs[j % 2][...] = jnp.dot(xb, w_ref[...], preferred_element_type=F32)

    def epilogue(j):
        h_ref = h_refs[j % 2]
        r = slice(j * SUB_IN, (j + 1) * SUB_IN)
        rows = pl.ds(pl.multiple_of(t0 + j * SUB_IN, SUB_IN), SUB_IN)

        cos, sin = ta_ref[0, rows, :], ta_ref[1, rows, :]
        for c in range(2):
            sl = slice(c * LANES, (c + 1) * LANES)
            hq = h_ref[:, OFF_AQ + c * LANES:OFF_AQ + (c + 1) * LANES]
            hk = h_ref[:, OFF_AK + c * LANES:OFF_AK + (c + 1) * LANES]
            aq_ref[r, sl] = (_rope128(hq, cos, sin) * a_scale).astype(BF16)
            akt_ref[sl, r] = _rope128(hk, cos, sin).T.astype(BF16)
        av_ref[r, :] = h_ref[:, OFF_AV:OFF_AV + 256].astype(BF16)

        cos, sin = tb_ref[0, rows, :], tb_ref[1, rows, :]
        gq = gq_ref[...]
        for c in range(B_WIDTH // LANES):
            sl = slice(c * LANES, (c + 1) * LANES)
            qn = _rms64(h_ref[:, OFF_BQ + c * LANES:OFF_BQ + (c + 1) * LANES], lo, gq, b_scale)
            bq_ref[r, sl] = _rope128(qn, cos, sin).astype(BF16)
        kn = _rms64(h_ref[:, OFF_BK:OFF_BK + B_KV_COLS], lo, gk_ref[...], 1.0)
        bkt_ref[:, r] = _rope128(kn, cos, sin).T.astype(BF16)
        bv_ref[r, :] = h_ref[:, OFF_BV:OFF_BV + B_KV_COLS].astype(BF16)

        cu = _gelu_tanh(h_ref[:, OFF_CU:OFF_CU + C_WIDTH])
        cv = _layer_norm(_gelu_tanh(h_ref[:, OFF_CV:OFF_CV + C_WIDTH]), clg_ref[...], clb_ref[...]).astype(BF16)
        for ci in range(SUB_IN // CHUNK):
            rs = slice(ci * CHUNK, (ci + 1) * CHUNK)
            vch = cv[rs, :]
            zero = jnp.zeros_like(vch)
            mixed = bsm_ref[...]
            for g in range(C_GROUPS):
                mixed = mixed + jnp.dot(ws_ref[g], jnp.where(group == g, vch, zero),
                                        preferred_element_type=F32)
            co_ref[j * SUB_IN + ci * CHUNK:j * SUB_IN + (ci + 1) * CHUNK, :] = (cu[rs, :] * mixed).astype(BF16)

    n_sub = tm // SUB_IN
    project(0)
    for j in range(n_sub):
        if j + 1 < n_sub:
            project(j + 1)
        epilogue(j)


def _in_call(x2d, w_in, ta, tb, gq, gk, clg, clb, ws, bsm, seq):
    m = x2d.shape[0]
    tm = TM_IN
    const2 = lambda i: (0, 0)
    const3 = lambda i: (0, 0, 0)
    row = lambda i: (i, 0)
    outs = ((256, False), (256, True), (256, False), (B_WIDTH, False), (B_KV_COLS, True),
            (B_KV_COLS, False), (C_WIDTH, False))
    col = lambda i: (0, i)
    return pl.pallas_call(
        functools.partial(_in_kernel, tm=tm, blocks_per_seq=seq // tm),
        grid=(m // tm,),
        in_specs=[
            pl.BlockSpec((tm, D_MODEL), row),
            pl.BlockSpec((D_MODEL, IN_COLS), const2, pipeline_mode=pl.Buffered(1)),
            pl.BlockSpec((2, seq, LANES), const3, pipeline_mode=pl.Buffered(1)),
            pl.BlockSpec((2, seq, LANES), const3, pipeline_mode=pl.Buffered(1)),
            pl.BlockSpec((1, LANES), const2),
            pl.BlockSpec((1, LANES), const2),
            pl.BlockSpec((1, C_WIDTH), const2),
            pl.BlockSpec((1, C_WIDTH), const2),
            pl.BlockSpec((C_GROUPS, CHUNK, CHUNK), const3),
            pl.BlockSpec((CHUNK, C_WIDTH), const2),
        ],
        out_specs=[pl.BlockSpec((w, tm), col) if t else pl.BlockSpec((tm, w), row) for w, t in outs],
        out_shape=[jax.ShapeDtypeStruct((w, m) if t else (m, w), BF16) for w, t in outs],
        scratch_shapes=[pltpu.VMEM((SUB_IN, IN_COLS), F32)] * 2,
        compiler_params=pltpu.CompilerParams(
            dimension_semantics=("arbitrary",), vmem_limit_bytes=VMEM_LIMIT_BYTES),
        name="in_proj",
    )(x2d, w_in, ta, tb, gq, gk, clg, clb, ws, bsm)


def _fill_v_ones(v_ref, v1_ref, col0):
    @pl.when(pl.program_id(1) == 0)
    def _():
        v1_ref[:, :LANES] = v_ref[:, col0:col0 + LANES]
        v1_ref[:, LANES:] = jnp.ones((v_ref.shape[0], LANES), BF16)


def _sweep_units(units, q_ref, kt_ref, s_refs, e_refs):
    def scores(u):
        qcol, krow, mask, _ = units[u]
        qc = q_ref[:, qcol * LANES:(qcol + 1) * LANES]
        lhs = jnp.where(mask, qc, jnp.zeros_like(qc))
        s_refs[u % 2][...] = jnp.dot(lhs, kt_ref[krow * LANES:(krow + 1) * LANES, :],
                                     preferred_element_type=F32)

    def probs(u):
        s = s_refs[u % 2][...]
        e_refs[u % 2][...] = jnp.exp2(s - jnp.max(s, axis=-1, keepdims=True)).astype(BF16)

    def values(u):
        o = jnp.dot(e_refs[u % 2][...], units[u][3][...], preferred_element_type=F32)
        return o[:, :LANES] / o[:, LANES:]

    outs = []
    scores(0)
    for u in range(len(units)):
        if u + 1 < len(units):
            scores(u + 1)
        probs(u)
        outs.append(values(u))
    return outs


SAFE_SHIFT_LOG2 = 40.0


def _attn_a_kernel(lq_ref, g_ref, q_ref, kt_ref, v_ref, o_ref, s0_ref, s1_ref, e0_ref, e1_ref,
                   va_ref, vb_ref, kn_ref, *, lam_init):
    lq = lq_ref[...]
    lam = (jnp.exp(jnp.sum(lq[0:1] * lq[1:2], axis=-1, keepdims=True))
           - jnp.exp(jnp.sum(lq[2:3] * lq[3:4], axis=-1, keepdims=True)) + lam_init)
    tq = q_ref.shape[0]
    half = tq // 2
    lane = lax.broadcasted_iota(jnp.int32, (tq, LANES), 1)
    half_lane = lax.broadcasted_iota(jnp.int32, (half, LANES), 1)
    v1_refs = (va_ref, vb_ref)
    for cb in range(2):
        _fill_v_ones(v_ref, v1_refs[cb], cb * LANES)

    def group_of(ln):
        return (ln % 64) // 16

    def group_mask(h, c, ln):
        return group_of(ln) == 2 * (h % 2) + c

    row_lane = lax.broadcasted_iota(jnp.int32, (1, LANES), 1)

    @pl.when(pl.program_id(1) == 0)
    def _():
        for cb in range(2):
            acc = jnp.zeros((1, LANES), F32)
            for g in range(4):
                parts = [kt_ref[cb * LANES + p * 64 + 16 * g:cb * LANES + p * 64 + 16 * (g + 1), :].astype(F32)
                         for p in range(2)]
                n2 = sum(jnp.sum(p * p, axis=0, keepdims=True) for p in parts)
                acc = jnp.where(group_of(row_lane) == g, jnp.max(n2, axis=-1, keepdims=True), acc)
            kn_ref[cb:cb + 1, :] = acc

    same_group = (group_of(lax.broadcasted_iota(jnp.int32, (LANES, LANES), 0))
                  == group_of(lax.broadcasted_iota(jnp.int32, (LANES, LANES), 1)))
    indicator = jnp.where(same_group, 1.0, 0.0).astype(BF16)
    shifts = []
    for cb in range(2):
        q2 = q_ref[:, cb * LANES:(cb + 1) * LANES].astype(F32)
        n2 = jnp.dot((q2 * q2).astype(BF16), indicator, preferred_element_type=F32)
        shifts.append(jnp.sqrt(jnp.max(n2, axis=0, keepdims=True) * kn_ref[cb:cb + 1, :]) * 1.01)
    worst = jnp.max(jnp.maximum(shifts[0], shifts[1]))

    def finish(heads):
        lo = lane < A_V_DIM
        for cb in range(A_WIDTH // LANES):
            o = jnp.where(lo, heads[2 * cb], heads[2 * cb + 1])
            o2 = o * o
            ms_lo = jnp.sum(jnp.where(lo, o2, 0.0), axis=-1, keepdims=True)
            ms_hi = jnp.sum(jnp.where(lo, 0.0, o2), axis=-1, keepdims=True)
            r = lax.rsqrt(jnp.where(lo, ms_lo, ms_hi) * (1.0 / A_V_DIM) + LN_EPS)
            o_ref[:, cb * LANES:(cb + 1) * LANES] = (o * r * g_ref[...] * (1.0 - lam_init)).astype(BF16)

    def one_pv_per_head():
        steps = [(h, r) for h in range(A_HEADS) for r in range(2)]
        s_refs, w_refs = (s0_ref, s1_ref), (e0_ref, e1_ref)

        def scores(k):
            h, r = steps[k]
            cb = h // 2
            qc = q_ref[r * half:(r + 1) * half, cb * LANES:(cb + 1) * LANES]
            zero = jnp.zeros_like(qc)
            lhs = jnp.concatenate([jnp.where(group_mask(h, c, half_lane), qc, zero) for c in range(2)], axis=0)
            s = jnp.dot(lhs, kt_ref[cb * LANES:(cb + 1) * LANES, :], preferred_element_type=F32)
            for c in range(2):
                lane0 = 16 * (2 * (h % 2) + c)
                s_refs[k % 2][c * half:(c + 1) * half, :] = jnp.exp2(
                    s[c * half:(c + 1) * half] - shifts[cb][:, lane0:lane0 + 1])

        def values(k):
            h, _ = steps[k]
            e = s_refs[k % 2][...]
            l = jnp.sum(e, axis=-1, keepdims=True)
            l1, l2 = l[:half], l[half:]
            w_refs[k % 2][0:half, :] = (e[:half] - (lam * l1 / l2) * e[half:]).astype(BF16)
            o = jnp.dot(w_refs[k % 2][0:half, :], v1_refs[h // 2][:, :LANES], preferred_element_type=F32)
            return o / l1

        parts = []
        scores(0)
        for k in range(len(steps)):
            if k + 1 < len(steps):
                scores(k + 1)
            parts.append(values(k))
        finish([jnp.concatenate(parts[2 * h:2 * h + 2], axis=0) for h in range(A_HEADS)])

    def two_pvs_per_head():
        units = [(h // 2, h // 2, group_mask(h, c, lane), v1_refs[h // 2])
                 for h in range(A_HEADS) for c in range(2)]
        outs = _sweep_units(units, q_ref, kt_ref, (s0_ref, s1_ref), (e0_ref, e1_ref))
        finish([outs[2 * h] - lam * outs[2 * h + 1] for h in range(A_HEADS)])

    lax.cond(worst <= SAFE_SHIFT_LOG2, one_pv_per_head, two_pvs_per_head)


def _attn_scratch(seq):
    return [pltpu.VMEM((TQ, seq), F32), pltpu.VMEM((TQ, seq), F32),
            pltpu.VMEM((TQ, seq), BF16), pltpu.VMEM((TQ, seq), BF16)]


def _attn_a_call(lq, g128, aq, akt, av, seq, lam_init):
    m = aq.shape[0]
    nq = seq // TQ
    const2 = lambda b, i: (0, 0)
    return pl.pallas_call(
        functools.partial(_attn_a_kernel, lam_init=lam_init),
        grid=(m // seq, nq),
        in_specs=[
            pl.BlockSpec((4, A_QK_DIM), const2),
            pl.BlockSpec((1, LANES), const2),
            pl.BlockSpec((TQ, A_WIDTH), lambda b, i: (b * nq + i, 0)),
            pl.BlockSpec((A_WIDTH, seq), lambda b, i: (0, b)),
            pl.BlockSpec((seq, A_WIDTH), lambda b, i: (b, 0)),
        ],
        out_specs=pl.BlockSpec((TQ, A_WIDTH), lambda b, i: (b * nq + i, 0)),
        out_shape=jax.ShapeDtypeStruct((m, A_WIDTH), BF16),
        scratch_shapes=_attn_scratch(seq) + [pltpu.VMEM((seq, 2 * LANES), BF16)] * 2
        + [pltpu.VMEM((8, LANES), F32)],
        compiler_params=pltpu.CompilerParams(
            dimension_semantics=("arbitrary", "arbitrary"), vmem_limit_bytes=VMEM_LIMIT_BYTES),
        name="attn_a",
    )(lq, g128, aq, akt, av)


def _attn_b_kernel(q_ref, kt_ref, v_ref, o_ref, s0_ref, s1_ref, e0_ref, e1_ref, v1_ref):
    tq = q_ref.shape[0]
    lane = lax.broadcasted_iota(jnp.int32, (tq, LANES), 1)
    lo = lane < B_HEAD_DIM
    first = (lane % B_HEAD_DIM) < B_HEAD_DIM // 2
    _fill_v_ones(v_ref, v1_ref, 0)
    units = [(c, 0, first if half == 0 else ~first, v1_ref)
             for c in range(B_WIDTH // LANES) for half in range(2)]
    outs = _sweep_units(units, q_ref, kt_ref, (s0_ref, s1_ref), (e0_ref, e1_ref))
    for c in range(B_WIDTH // LANES):
        o_ref[:, c * LANES:(c + 1) * LANES] = jnp.where(lo, outs[2 * c], outs[2 * c + 1]).astype(BF16)


def _attn_b_call(bq, bkt, bv, seq):
    m = bq.shape[0]
    nq = seq // TQ
    return pl.pallas_call(
        _attn_b_kernel,
        grid=(m // seq, nq),
        in_specs=[
            pl.BlockSpec((TQ, B_WIDTH), lambda b, i: (b * nq + i, 0)),
            pl.BlockSpec((B_KV_COLS, seq), lambda b, i: (0, b)),
            pl.BlockSpec((seq, B_KV_COLS), lambda b, i: (b, 0)),
        ],
        out_specs=pl.BlockSpec((TQ, B_WIDTH), lambda b, i: (b * nq + i, 0)),
        out_shape=jax.ShapeDtypeStruct((m, B_WIDTH), BF16),
        scratch_shapes=_attn_scratch(seq) + [pltpu.VMEM((seq, 2 * LANES), BF16)],
        compiler_params=pltpu.CompilerParams(
            dimension_semantics=("arbitrary", "arbitrary"), vmem_limit_bytes=VMEM_LIMIT_BYTES),
        name="attn_b",
    )(bq, bkt, bv)


def _post_kernel(x_ref, a_ref, b_ref, c_ref, wa_ref, wb_ref, wc_ref, g1_ref, b1_ref,
                 wg_ref, wu_ref, wd_ref, g2_ref, b2_ref, o_ref):
    tm = x_ref.shape[0]
    n_sub = tm // SUB_POST
    rows = [slice(j * SUB_POST, (j + 1) * SUB_POST) for j in range(n_sub)]

    def mix(r):
        return (jnp.dot(a_ref[r, :], wa_ref[...], preferred_element_type=F32)
                + jnp.dot(b_ref[r, :], wb_ref[...], preferred_element_type=F32)
                + jnp.dot(c_ref[r, :], wc_ref[...], preferred_element_type=F32))

    def ln1(r, m):
        return _layer_norm(DEEPNORM_ALPHA * x_ref[r, :] + m, g1_ref[...], b1_ref[...])

    def hidden(y):
        yb = y.astype(BF16)
        gate = jnp.dot(yb, wg_ref[...], preferred_element_type=F32)
        up = jnp.dot(yb, wu_ref[...], preferred_element_type=F32)
        return (gate * jax.nn.sigmoid(gate) * up).astype(BF16)

    def down(hid):
        return jnp.dot(hid, wd_ref[...], preferred_element_type=F32)

    def ln2(r, y, f):
        o_ref[r, :] = _layer_norm(DEEPNORM_ALPHA * y + f, g2_ref[...], b2_ref[...])

    mixes = [mix(r) for r in rows]
    ys = [None] * n_sub
    ys[0] = ln1(rows[0], mixes[0])
    ffn_prev = None
    for j in range(n_sub):
        hid = hidden(ys[j])
        if j + 1 < n_sub:
            ys[j + 1] = ln1(rows[j + 1], mixes[j + 1])
        if j > 0:
            ln2(rows[j - 1], ys[j - 1], ffn_prev)
        ffn_prev = down(hid)
    ln2(rows[-1], ys[-1], ffn_prev)


def _post_call(x2d, ao, bo, co, wa, wb, wc, g1, b1, wg, wu, wd, g2, b2):
    m = x2d.shape[0]
    tm = TM_POST
    row = lambda i: (i, 0)
    const2 = lambda i: (0, 0)

    def resident(shape):
        return pl.BlockSpec(shape, const2, pipeline_mode=pl.Buffered(1))

    return pl.pallas_call(
        _post_kernel,
        grid=(m // tm,),
        in_specs=[
            pl.BlockSpec((tm, D_MODEL), row),
            pl.BlockSpec((tm, A_WIDTH), row),
            pl.BlockSpec((tm, B_WIDTH), row),
            pl.BlockSpec((tm, C_WIDTH), row),
            resident((A_WIDTH, D_MODEL)),
            resident((B_WIDTH, D_MODEL)),
            resident((C_WIDTH, D_MODEL)),
            resident((1, D_MODEL)),
            resident((1, D_MODEL)),
            resident((D_MODEL, FFN_HIDDEN)),
            resident((D_MODEL, FFN_HIDDEN)),
            resident((FFN_HIDDEN, D_MODEL)),
            resident((1, D_MODEL)),
            resident((1, D_MODEL)),
        ],
        out_specs=pl.BlockSpec((tm, D_MODEL), row),
        out_shape=jax.ShapeDtypeStruct((m, D_MODEL), F32),
        compiler_params=pltpu.CompilerParams(
            dimension_semantics=("arbitrary",), vmem_limit_bytes=VMEM_LIMIT_BYTES),
        name="post",
    )(x2d, ao, bo, co, wa, wb, wc, g1, b1, wg, wu, wd, g2, b2)


def _rope_tables(pos, dim):
    inv = 1.0 / (ROPE_THETA ** (jnp.arange(0, dim, 2, dtype=F32) / dim))
    ang = pos.astype(F32)[:, None] * inv[None, :]
    ang = jnp.concatenate([ang, ang], axis=-1)
    return jnp.cos(ang), jnp.sin(ang)


def _lane_dims():
    lane = np.arange(LANES)
    half, g, j = lane // 64, (lane % 64) // 16, lane % 16
    return g, 16 * half + j, half


def _rope_block_tables(cos_groups, sin_groups):
    g, d, half = _lane_dims()
    cos = jnp.stack([cos_groups[gi][:, di] for gi, di in zip(g, d)], axis=1)
    sin = jnp.stack([sin_groups[gi][:, di] for gi, di in zip(g, d)], axis=1)
    return jnp.stack([cos, jnp.where(half == 0, -sin, sin)])


def kernel(x, w_in, w_out, lam_qk, a_subln_g, b_q_norm_g, b_k_norm_g, c_ln_g, c_ln_b, c_w_s, c_b_s,
           ln1_g, ln1_b, w_gate, w_up, w_down, ln2_g, ln2_b):
    bsz, seq, _ = x.shape
    assert seq % TM_IN == 0 and seq % TQ == 0 and seq % GRID_W == 0 and (bsz * seq) % TM_POST == 0
    t = jnp.arange(seq, dtype=jnp.int32)
    cos_a, sin_a = _rope_tables(t, A_QK_DIM)
    cos_r, sin_r = _rope_tables(t // GRID_W, B_HEAD_DIM // 2)
    cos_c, sin_c = _rope_tables(t % GRID_W, B_HEAD_DIM // 2)
    ta = _rope_block_tables([cos_a] * 4, [sin_a] * 4)
    tb = _rope_block_tables([cos_r, cos_c] * 2, [sin_r, sin_c] * 2)

    g, d, _ = _lane_dims()
    block_perm = 32 * g + d
    in_perm = np.arange(IN_COLS)
    for off, width in ((OFF_AQ, 256), (OFF_AK, 256), (OFF_BK, B_KV_COLS)):
        for c in range(width // LANES):
            in_perm[off + c * LANES:off + (c + 1) * LANES] = off + c * LANES + block_perm
    for c in range(B_WIDTH // LANES):
        in_perm[OFF_BQ + c * LANES:OFF_BQ + (c + 1) * LANES] = OFF_BQ + _BQ_PERM[c * LANES + block_perm]
    out_perm = np.arange(D_MODEL)
    out_perm[A_WIDTH:A_WIDTH + B_WIDTH] = A_WIDTH + _BQ_PERM
    gain_perm = block_perm % B_HEAD_DIM
    w_in_b = w_in[:, :, in_perm].astype(BF16)
    w_out_b = w_out[:, out_perm, :].astype(BF16)
    w_gate_b, w_up_b, w_down_b = w_gate.astype(BF16), w_up.astype(BF16), w_down.astype(BF16)
    w_s_b = c_w_s.astype(BF16)

    x2d = x.reshape(bsz * seq, D_MODEL)
    for l in range(DEPTH):
        lam_init = 0.8 - 0.6 * math.exp(-0.3 * l)
        wa = w_out_b[l, :A_WIDTH]
        wb = w_out_b[l, A_WIDTH:A_WIDTH + B_WIDTH]
        wc = w_out_b[l, A_WIDTH + B_WIDTH:]
        bsm = jnp.repeat(c_b_s[l].T, C_GROUP_DIM, axis=1)
        aq, akt, av, bq, bkt, bv, co = _in_call(
            x2d, w_in_b[l], ta, tb,
            b_q_norm_g[l][gain_perm][None], b_k_norm_g[l][gain_perm][None],
            c_ln_g[l][None], c_ln_b[l][None], w_s_b[l], bsm, seq)
        ao = _attn_a_call(lam_qk[l], jnp.tile(a_subln_g[l], 2)[None], aq, akt, av, seq, lam_init)
        bo = _attn_b_call(bq, bkt, bv, seq)
        x2d = _post_call(x2d, ao, bo, co, wa, wb, wc, ln1_g[l][None], ln1_b[l][None],
                         w_gate_b[l], w_up_b[l], w_down_b[l], ln2_g[l][None], ln2_b[l][None])
    return x2d.reshape(bsz, seq, D_MODEL)
```

```python
import functools
import math

import jax
import jax.numpy as jnp
import numpy as np
from jax import lax
from jax.experimental import pallas as pl
from jax.experimental.pallas import tpu as pltpu

D_MODEL = 1024
DEPTH = 4
GRID_W = 64
ROPE_THETA = 10000.0
CHUNK = 128
LN_EPS = 1e-5

A_HEADS = 4
A_QK_DIM = 32
A_V_DIM = 64
A_WIDTH = A_HEADS * A_V_DIM
B_HEADS = 8
B_KV_HEADS = 2
B_HEAD_DIM = 64
B_GROUP = B_HEADS // B_KV_HEADS
B_WIDTH = B_HEADS * B_HEAD_DIM
B_KV_COLS = B_KV_HEADS * B_HEAD_DIM
C_GROUPS = 4
C_GROUP_DIM = 64
C_WIDTH = C_GROUPS * C_GROUP_DIM
IN_COLS = 2048
FFN_HIDDEN = 2816
DEEPNORM_ALPHA = (2 * DEPTH) ** 0.25

LANES = 128
LOG2E = 1.4426950408889634
VMEM_LIMIT_BYTES = 56 * 1024 * 1024

OFF_AQ, OFF_AK, OFF_AV = 0, 256, 512
OFF_BQ, OFF_BK, OFF_BV = 768, 1280, 1408
OFF_CU, OFF_CV = 1536, 1792

TM_IN = 1024
SUB_IN = 256
TQ = 1024
TQ_A = 512
TM_POST = 512
SUB_POST = 256

F32 = jnp.float32
BF16 = jnp.bfloat16


def _bq_permutation():
    n = np.arange(B_WIDTH)
    head = n // LANES + B_GROUP * ((n % LANES) // B_HEAD_DIM)
    return head * B_HEAD_DIM + n % B_HEAD_DIM


_BQ_PERM = _bq_permutation()


def _gelu_tanh(x):
    c = math.sqrt(2.0 / math.pi)
    return x * (0.5 * (1.0 + jnp.tanh(c * (x + 0.044715 * (x * x * x)))))


def _layer_norm(x, g, b):
    mu = jnp.mean(x, axis=-1, keepdims=True)
    xc = x - mu
    var = jnp.mean(xc * xc, axis=-1, keepdims=True)
    return xc * lax.rsqrt(var + LN_EPS) * g + b


def _rope128(xb, cos, sin_signed):
    return xb * cos + pltpu.roll(xb, LANES // 2, 1) * sin_signed


def _rms64(xb, lo, g128, post_scale):
    x2 = xb * xb
    s_lo = jnp.sum(jnp.where(lo, x2, 0.0), axis=-1, keepdims=True)
    s_hi = jnp.sum(jnp.where(lo, 0.0, x2), axis=-1, keepdims=True)
    r = lax.rsqrt(jnp.where(lo, s_lo, s_hi) * (1.0 / B_HEAD_DIM) + LN_EPS)
    return xb * r * g128 * post_scale


def _in_kernel(x_ref, w_ref, ta_ref, tb_ref, gq_ref, gk_ref, clg_ref, clb_ref, ws_ref, bsm_ref,
               aq_ref, akt_ref, av_ref, bq_ref, bkt_ref, bv_ref, co_ref, h0_ref, h1_ref,
               *, tm, blocks_per_seq):
    i = pl.program_id(0)
    t0 = (i % blocks_per_seq) * tm
    lane = lax.broadcasted_iota(jnp.int32, (SUB_IN, LANES), 1)
    lo = (lane % B_HEAD_DIM) < B_HEAD_DIM // 2
    group = lax.broadcasted_iota(jnp.int32, (CHUNK, C_WIDTH), 1) // C_GROUP_DIM
    a_scale = A_QK_DIM ** -0.5 * LOG2E
    b_scale = B_HEAD_DIM ** -0.5 * LOG2E
    h_refs = (h0_ref, h1_ref)

    def project(j):
        xb = x_ref[j * SUB_IN:(j + 1) * SUB_IN, :].astype(BF16)
        h_refs[j % 2][...] = jnp.dot(xb, w_ref[...], preferred_element_type=F32)

    def epilogue(j):
        h_ref = h_refs[j % 2]
        r = slice(j * SUB_IN, (j + 1) * SUB_IN)
        rows = pl.ds(pl.multiple_of(t0 + j * SUB_IN, SUB_IN), SUB_IN)

        cos, sin = ta_ref[0, rows, :], ta_ref[1, rows, :]
        for c in range(2):
            sl = slice(c * LANES, (c + 1) * LANES)
            hq = h_ref[:, OFF_AQ + c * LANES:OFF_AQ + (c + 1) * LANES]
            hk = h_ref[:, OFF_AK + c * LANES:OFF_AK + (c + 1) * LANES]
            aq_ref[r, sl] = (_rope128(hq, cos, sin) * a_scale).astype(BF16)
            akt_ref[sl, r] = _rope128(hk, cos, sin).T.astype(BF16)
        av_ref[r, :] = h_ref[:, OFF_AV:OFF_AV + 256].astype(BF16)

        cos, sin = tb_ref[0, rows, :], tb_ref[1, rows, :]
        gq = gq_ref[...]
        for c in range(B_WIDTH // LANES):
            sl = slice(c * LANES, (c + 1) * LANES)
            qn = _rms64(h_ref[:, OFF_BQ + c * LANES:OFF_BQ + (c + 1) * LANES], lo, gq, b_scale)
            bq_ref[r, sl] = _rope128(qn, cos, sin).astype(BF16)
        kn = _rms64(h_ref[:, OFF_BK:OFF_BK + B_KV_COLS], lo, gk_ref[...], 1.0)
        bkt_ref[:, r] = _rope128(kn, cos, sin).T.astype(BF16)
        bv_ref[r, :] = h_ref[:, OFF_BV:OFF_BV + B_KV_COLS].astype(BF16)

        cu = _gelu_tanh(h_ref[:, OFF_CU:OFF_CU + C_WIDTH])
        cv = _layer_norm(_gelu_tanh(h_ref[:, OFF_CV:OFF_CV + C_WIDTH]), clg_ref[...], clb_ref[...]).astype(BF16)
        for ci in range(SUB_IN // CHUNK):
            rs = slice(ci * CHUNK, (ci + 1) * CHUNK)
            vch = cv[rs, :]
            zero = jnp.zeros_like(vch)
            mixed = bsm_ref[...]
            for g in range(C_GROUPS):
                mixed = mixed + jnp.dot(ws_ref[g], jnp.where(group == g, vch, zero),
                                        preferred_element_type=F32)
            co_ref[j * SUB_IN + ci * CHUNK:j * SUB_IN + (ci + 1) * CHUNK, :] = (cu[rs, :] * mixed).astype(BF16)

    n_sub = tm // SUB_IN
    project(0)
    for j in range(n_sub):
        if j + 1 < n_sub:
            project(j + 1)
        epilogue(j)


def _in_call(x2d, w_in, ta, tb, gq, gk, clg, clb, ws, bsm, seq):
    m = x2d.shape[0]
    tm = TM_IN
    const2 = lambda i: (0, 0)
    const3 = lambda i: (0, 0, 0)
    row = lambda i: (i, 0)
    outs = ((256, False), (256, True), (256, False), (B_WIDTH, False), (B_KV_COLS, True),
            (B_KV_COLS, False), (C_WIDTH, False))
    col = lambda i: (0, i)
    return pl.pallas_call(
        functools.partial(_in_kernel, tm=tm, blocks_per_seq=seq // tm),
        grid=(m // tm,),
        in_specs=[
            pl.BlockSpec((tm, D_MODEL), row),
            pl.BlockSpec((D_MODEL, IN_COLS), const2, pipeline_mode=pl.Buffered(1)),
            pl.BlockSpec((2, seq, LANES), const3, pipeline_mode=pl.Buffered(1)),
            pl.BlockSpec((2, seq, LANES), const3, pipeline_mode=pl.Buffered(1)),
            pl.BlockSpec((1, LANES), const2),
            pl.BlockSpec((1, LANES), const2),
            pl.BlockSpec((1, C_WIDTH), const2),
            pl.BlockSpec((1, C_WIDTH), const2),
            pl.BlockSpec((C_GROUPS, CHUNK, CHUNK), const3),
            pl.BlockSpec((CHUNK, C_WIDTH), const2),
        ],
        out_specs=[pl.BlockSpec((w, tm), col) if t else pl.BlockSpec((tm, w), row) for w, t in outs],
        out_shape=[jax.ShapeDtypeStruct((w, m) if t else (m, w), BF16) for w, t in outs],
        scratch_shapes=[pltpu.VMEM((SUB_IN, IN_COLS), F32)] * 2,
        compiler_params=pltpu.CompilerParams(
            dimension_semantics=("arbitrary",), vmem_limit_bytes=VMEM_LIMIT_BYTES),
        name="in_proj",
    )(x2d, w_in, ta, tb, gq, gk, clg, clb, ws, bsm)


def _fill_v_ones(v_ref, v1_ref, col0):
    @pl.when(pl.program_id(1) == 0)
    def _():
        v1_ref[:, :LANES] = v_ref[:, col0:col0 + LANES]
        v1_ref[:, LANES:] = jnp.ones((v_ref.shape[0], LANES), BF16)


def _sweep_units(units, q_ref, kt_ref, s_refs, e_refs):
    def scores(u):
        qcol, krow, mask, _ = units[u]
        qc = q_ref[:, qcol * LANES:(qcol + 1) * LANES]
        lhs = jnp.where(mask, qc, jnp.zeros_like(qc))
        s_refs[u % 2][...] = jnp.dot(lhs, kt_ref[krow * LANES:(krow + 1) * LANES, :],
                                     preferred_element_type=F32)

    def probs(u):
        s = s_refs[u % 2][...]
        e_refs[u % 2][...] = jnp.exp2(s - jnp.max(s, axis=-1, keepdims=True)).astype(BF16)

    def values(u):
        o = jnp.dot(e_refs[u % 2][...], units[u][3][...], preferred_element_type=F32)
        return o[:, :LANES] / o[:, LANES:]

    outs = []
    scores(0)
    for u in range(len(units)):
        if u + 1 < len(units):
            scores(u + 1)
        probs(u)
        outs.append(values(u))
    return outs


SAFE_SHIFT_LOG2 = 40.0


def _attn_a_kernel(lq_ref, g_ref, q_ref, kt_ref, v_ref, o_ref, s0_ref, s1_ref, e0_ref, e1_ref,
                   va_ref, vb_ref, kn_ref, *, lam_init):
    lq = lq_ref[...]
    lam = (jnp.exp(jnp.sum(lq[0:1] * lq[1:2], axis=-1, keepdims=True))
           - jnp.exp(jnp.sum(lq[2:3] * lq[3:4], axis=-1, keepdims=True)) + lam_init)
    tq = q_ref.shape[0]
    half = tq // 2
    lane = lax.broadcasted_iota(jnp.int32, (tq, LANES), 1)
    half_lane = lax.broadcasted_iota(jnp.int32, (half, LANES), 1)
    v1_refs = (va_ref, vb_ref)
    for cb in range(2):
        _fill_v_ones(v_ref, v1_refs[cb], cb * LANES)

    def group_of(ln):
        return (ln % 64) // 16

    def group_mask(h, c, ln):
        return group_of(ln) == 2 * (h % 2) + c

    row_lane = lax.broadcasted_iota(jnp.int32, (1, LANES), 1)

    @pl.when(pl.program_id(1) == 0)
    def _():
        for cb in range(2):
            acc = jnp.zeros((1, LANES), F32)
            for g in range(4):
                parts = [kt_ref[cb * LANES + p * 64 + 16 * g:cb * LANES + p * 64 + 16 * (g + 1), :].astype(F32)
                         for p in range(2)]
                n2 = sum(jnp.sum(p * p, axis=0, keepdims=True) for p in parts)
                acc = jnp.where(group_of(row_lane) == g, jnp.max(n2, axis=-1, keepdims=True), acc)
            kn_ref[cb:cb + 1, :] = acc

    same_group = (group_of(lax.broadcasted_iota(jnp.int32, (LANES, LANES), 0))
                  == group_of(lax.broadcasted_iota(jnp.int32, (LANES, LANES), 1)))
    indicator = jnp.where(same_group, 1.0, 0.0).astype(BF16)
    shifts = []
    for cb in range(2):
        q2 = q_ref[:, cb * LANES:(cb + 1) * LANES].astype(F32)
        n2 = jnp.dot((q2 * q2).astype(BF16), indicator, preferred_element_type=F32)
        shifts.append(jnp.sqrt(jnp.max(n2, axis=0, keepdims=True) * kn_ref[cb:cb + 1, :]) * 1.01)
    worst = jnp.max(jnp.maximum(shifts[0], shifts[1]))

    def finish(heads):
        lo = lane < A_V_DIM
        for cb in range(A_WIDTH // LANES):
            o = jnp.where(lo, heads[2 * cb], heads[2 * cb + 1])
            o2 = o * o
            ms_lo = jnp.sum(jnp.where(lo, o2, 0.0), axis=-1, keepdims=True)
            ms_hi = jnp.sum(jnp.where(lo, 0.0, o2), axis=-1, keepdims=True)
            r = lax.rsqrt(jnp.where(lo, ms_lo, ms_hi) * (1.0 / A_V_DIM) + LN_EPS)
            o_ref[:, cb * LANES:(cb + 1) * LANES] = (o * r * g_ref[...] * (1.0 - lam_init)).astype(BF16)

    def one_pv_per_head():
        steps = [(h, r) for h in range(A_HEADS) for r in range(2)]
        s_refs, w_refs = (s0_ref, s1_ref), (e0_ref, e1_ref)

        def scores(k):
            h, r = steps[k]
            cb = h // 2
            qc = q_ref[r * half:(r + 1) * half, cb * LANES:(cb + 1) * LANES]
            zero = jnp.zeros_like(qc)
            lhs = jnp.concatenate([jnp.where(group_mask(h, c, half_lane), qc, zero) for c in range(2)], axis=0)
            s = jnp.dot(lhs, kt_ref[cb * LANES:(cb + 1) * LANES, :], preferred_element_type=F32)
            for c in range(2):
                lane0 = 16 * (2 * (h % 2) + c)
                s_refs[k % 2][c * half:(c + 1) * half, :] = jnp.exp2(
                    s[c * half:(c + 1) * half] - shifts[cb][:, lane0:lane0 + 1])

        def values(k):
            h, _ = steps[k]
            e = s_refs[k % 2][...]
            l = jnp.sum(e, axis=-1, keepdims=True)
            l1, l2 = l[:half], l[half:]
            w_refs[k % 2][0:half, :] = (e[:half] - (lam * l1 / l2) * e[half:]).astype(BF16)
            o = jnp.dot(w_refs[k % 2][0:half, :], v1_refs[h // 2][:, :LANES], preferred_element_type=F32)
            return o / l1

        parts = []
        scores(0)
        for k in range(len(steps)):
            if k + 1 < len(steps):
                scores(k + 1)
            parts.append(values(k))
        finish([jnp.concatenate(parts[2 * h:2 * h + 2], axis=0) for h in range(A_HEADS)])

    def two_pvs_per_head():
        units = [(h // 2, h // 2, group_mask(h, c, lane), v1_refs[h // 2])
                 for h in range(A_HEADS) for c in range(2)]
        outs = _sweep_units(units, q_ref, kt_ref, (s0_ref, s1_ref), (e0_ref, e1_ref))
        finish([outs[2 * h] - lam * outs[2 * h + 1] for h in range(A_HEADS)])

    lax.cond(worst <= SAFE_SHIFT_LOG2, one_pv_per_head, two_pvs_per_head)


def _attn_scratch(tq, seq):
    return [pltpu.VMEM((tq, seq), F32), pltpu.VMEM((tq, seq), F32),
            pltpu.VMEM((tq, seq), BF16), pltpu.VMEM((tq, seq), BF16)]


def _attn_a_call(lq, g128, aq, akt, av, seq, lam_init):
    m = aq.shape[0]
    nq = seq // TQ_A
    const2 = lambda b, i: (0, 0)
    return pl.pallas_call(
        functools.partial(_attn_a_kernel, lam_init=lam_init),
        grid=(m // seq, nq),
        in_specs=[
            pl.BlockSpec((4, A_QK_DIM), const2),
            pl.BlockSpec((1, LANES), const2),
            pl.BlockSpec((TQ_A, A_WIDTH), lambda b, i: (b * nq + i, 0)),
            pl.BlockSpec((A_WIDTH, seq), lambda b, i: (0, b)),
            pl.BlockSpec((seq, A_WIDTH), lambda b, i: (b, 0)),
        ],
        out_specs=pl.BlockSpec((TQ_A, A_WIDTH), lambda b, i: (b * nq + i, 0)),
        out_shape=jax.ShapeDtypeStruct((m, A_WIDTH), BF16),
        scratch_shapes=_attn_scratch(TQ_A, seq) + [pltpu.VMEM((seq, 2 * LANES), BF16)] * 2
        + [pltpu.VMEM((8, LANES), F32)],
        compiler_params=pltpu.CompilerParams(
            dimension_semantics=("arbitrary", "arbitrary"), vmem_limit_bytes=VMEM_LIMIT_BYTES),
        name="attn_a",
    )(lq, g128, aq, akt, av)


def _attn_b_kernel(q_ref, kt_ref, v_ref, o_ref, s0_ref, s1_ref, e0_ref, e1_ref, v1_ref):
    tq = q_ref.shape[0]
    lane = lax.broadcasted_iota(jnp.int32, (tq, LANES), 1)
    lo = lane < B_HEAD_DIM
    first = (lane % B_HEAD_DIM) < B_HEAD_DIM // 2
    _fill_v_ones(v_ref, v1_ref, 0)
    units = [(c, 0, first if half == 0 else ~first, v1_ref)
             for c in range(B_WIDTH // LANES) for half in range(2)]
    outs = _sweep_units(units, q_ref, kt_ref, (s0_ref, s1_ref), (e0_ref, e1_ref))
    for c in range(B_WIDTH // LANES):
        o_ref[:, c * LANES:(c + 1) * LANES] = jnp.where(lo, outs[2 * c], outs[2 * c + 1]).astype(BF16)


def _attn_b_call(bq, bkt, bv, seq):
    m = bq.shape[0]
    nq = seq // TQ
    return pl.pallas_call(
        _attn_b_kernel,
        grid=(m // seq, nq),
        in_specs=[
            pl.BlockSpec((TQ, B_WIDTH), lambda b, i: (b * nq + i, 0)),
            pl.BlockSpec((B_KV_COLS, seq), lambda b, i: (0, b)),
            pl.BlockSpec((seq, B_KV_COLS), lambda b, i: (b, 0)),
        ],
        out_specs=pl.BlockSpec((TQ, B_WIDTH), lambda b, i: (b * nq + i, 0)),
        out_shape=jax.ShapeDtypeStruct((m, B_WIDTH), BF16),
        scratch_shapes=_attn_scratch(TQ, seq) + [pltpu.VMEM((seq, 2 * LANES), BF16)],
        compiler_params=pltpu.CompilerParams(
            dimension_semantics=("arbitrary", "arbitrary"), vmem_limit_bytes=VMEM_LIMIT_BYTES),
        name="attn_b",
    )(bq, bkt, bv)


def _post_kernel(x_ref, a_ref, b_ref, c_ref, wa_ref, wb_ref, wc_ref, g1_ref, b1_ref,
                 wg_ref, wu_ref, wd_ref, g2_ref, b2_ref, o_ref):
    tm = x_ref.shape[0]
    n_sub = tm // SUB_POST
    rows = [slice(j * SUB_POST, (j + 1) * SUB_POST) for j in range(n_sub)]

    def mix(r):
        return (jnp.dot(a_ref[r, :], wa_ref[...], preferred_element_type=F32)
                + jnp.dot(b_ref[r, :], wb_ref[...], preferred_element_type=F32)
                + jnp.dot(c_ref[r, :], wc_ref[...], preferred_element_type=F32))

    def ln1(r, m):
        return _layer_norm(DEEPNORM_ALPHA * x_ref[r, :] + m, g1_ref[...], b1_ref[...])

    def hidden(y):
        yb = y.astype(BF16)
        gate = jnp.dot(yb, wg_ref[...], preferred_element_type=F32)
        up = jnp.dot(yb, wu_ref[...], preferred_element_type=F32)
        return (gate * jax.nn.sigmoid(gate) * up).astype(BF16)

    def down(hid):
        return jnp.dot(hid, wd_ref[...], preferred_element_type=F32)

    def ln2(r, y, f):
        o_ref[r, :] = _layer_norm(DEEPNORM_ALPHA * y + f, g2_ref[...], b2_ref[...])

    mixes = [mix(r) for r in rows]
    ys = [None] * n_sub
    ys[0] = ln1(rows[0], mixes[0])
    ffn_prev = None
    for j in range(n_sub):
        hid = hidden(ys[j])
        if j + 1 < n_sub:
            ys[j + 1] = ln1(rows[j + 1], mixes[j + 1])
        if j > 0:
            ln2(rows[j - 1], ys[j - 1], ffn_prev)
        ffn_prev = down(hid)
    ln2(rows[-1], ys[-1], ffn_prev)


def _post_call(x2d, ao, bo, co, wa, wb, wc, g1, b1, wg, wu, wd, g2, b2):
    m = x2d.shape[0]
    tm = TM_POST
    row = lambda i: (i, 0)
    const2 = lambda i: (0, 0)

    def resident(shape):
        return pl.BlockSpec(shape, const2, pipeline_mode=pl.Buffered(1))

    return pl.pallas_call(
        _post_kernel,
        grid=(m // tm,),
        in_specs=[
            pl.BlockSpec((tm, D_MODEL), row),
            pl.BlockSpec((tm, A_WIDTH), row),
            pl.BlockSpec((tm, B_WIDTH), row),
            pl.BlockSpec((tm, C_WIDTH), row),
            resident((A_WIDTH, D_MODEL)),
            resident((B_WIDTH, D_MODEL)),
            resident((C_WIDTH, D_MODEL)),
            resident((1, D_MODEL)),
            resident((1, D_MODEL)),
            resident((D_MODEL, FFN_HIDDEN)),
            resident((D_MODEL, FFN_HIDDEN)),
            resident((FFN_HIDDEN, D_MODEL)),
            resident((1, D_MODEL)),
            resident((1, D_MODEL)),
        ],
        out_specs=pl.BlockSpec((tm, D_MODEL), row),
        out_shape=jax.ShapeDtypeStruct((m, D_MODEL), F32),
        compiler_params=pltpu.CompilerParams(
            dimension_semantics=("arbitrary",), vmem_limit_bytes=VMEM_LIMIT_BYTES),
        name="post",
    )(x2d, ao, bo, co, wa, wb, wc, g1, b1, wg, wu, wd, g2, b2)


def _rope_tables(pos, dim):
    inv = 1.0 / (ROPE_THETA ** (jnp.arange(0, dim, 2, dtype=F32) / dim))
    ang = pos.astype(F32)[:, None] * inv[None, :]
    ang = jnp.concatenate([ang, ang], axis=-1)
    return jnp.cos(ang), jnp.sin(ang)


def _lane_dims():
    lane = np.arange(LANES)
    half, g, j = lane // 64, (lane % 64) // 16, lane % 16
    return g, 16 * half + j, half


def _rope_block_tables(cos_groups, sin_groups):
    g, d, half = _lane_dims()
    cos = jnp.stack([cos_groups[gi][:, di] for gi, di in zip(g, d)], axis=1)
    sin = jnp.stack([sin_groups[gi][:, di] for gi, di in zip(g, d)], axis=1)
    return jnp.stack([cos, jnp.where(half == 0, -sin, sin)])


def kernel(x, w_in, w_out, lam_qk, a_subln_g, b_q_norm_g, b_k_norm_g, c_ln_g, c_ln_b, c_w_s, c_b_s,
           ln1_g, ln1_b, w_gate, w_up, w_down, ln2_g, ln2_b):
    bsz, seq, _ = x.shape
    assert seq % TM_IN == 0 and seq % TQ == 0 and seq % GRID_W == 0 and (bsz * seq) % TM_POST == 0
    t = jnp.arange(seq, dtype=jnp.int32)
    cos_a, sin_a = _rope_tables(t, A_QK_DIM)
    cos_r, sin_r = _rope_tables(t // GRID_W, B_HEAD_DIM // 2)
    cos_c, sin_c = _rope_tables(t % GRID_W, B_HEAD_DIM // 2)
    ta = _rope_block_tables([cos_a] * 4, [sin_a] * 4)
    tb = _rope_block_tables([cos_r, cos_c] * 2, [sin_r, sin_c] * 2)

    g, d, _ = _lane_dims()
    block_perm = 32 * g + d
    in_perm = np.arange(IN_COLS)
    for off, width in ((OFF_AQ, 256), (OFF_AK, 256), (OFF_BK, B_KV_COLS)):
        for c in range(width // LANES):
            in_perm[off + c * LANES:off + (c + 1) * LANES] = off + c * LANES + block_perm
    for c in range(B_WIDTH // LANES):
        in_perm[OFF_BQ + c * LANES:OFF_BQ + (c + 1) * LANES] = OFF_BQ + _BQ_PERM[c * LANES + block_perm]
    out_perm = np.arange(D_MODEL)
    out_perm[A_WIDTH:A_WIDTH + B_WIDTH] = A_WIDTH + _BQ_PERM
    gain_perm = block_perm % B_HEAD_DIM
    w_in_b = w_in[:, :, in_perm].astype(BF16)
    w_out_b = w_out[:, out_perm, :].astype(BF16)
    w_gate_b, w_up_b, w_down_b = w_gate.astype(BF16), w_up.astype(BF16), w_down.astype(BF16)
    w_s_b = c_w_s.astype(BF16)

    x2d = x.reshape(bsz * seq, D_MODEL)
    for l in range(DEPTH):
        lam_init = 0.8 - 0.6 * math.exp(-0.3 * l)
        wa = w_out_b[l, :A_WIDTH]
        wb = w_out_b[l, A_WIDTH:A_WIDTH + B_WIDTH]
        wc = w_out_b[l, A_WIDTH + B_WIDTH:]
        bsm = jnp.repeat(c_b_s[l].T, C_GROUP_DIM, axis=1)
        aq, akt, av, bq, bkt, bv, co = _in_call(
            x2d, w_in_b[l], ta, tb,
            b_q_norm_g[l][gain_perm][None], b_k_norm_g[l][gain_perm][None],
            c_ln_g[l][None], c_ln_b[l][None], w_s_b[l], bsm, seq)
        ao = _attn_a_call(lam_qk[l], jnp.tile(a_subln_g[l], 2)[None], aq, akt, av, seq, lam_init)
        bo = _attn_b_call(bq, bkt, bv, seq)
        x2d = _post_call(x2d, ao, bo, co, wa, wb, wc, ln1_g[l][None], ln1_b[l][None],
                         w_gate_b[l], w_up_b[l], w_down_b[l], ln2_g[l][None], ln2_b[l][None])
    return x2d.reshape(bsz, seq, D_MODEL)
```

```python
import functools
import math

import jax
import jax.numpy as jnp
import numpy as np
from jax import lax
from jax.experimental import pallas as pl
from jax.experimental.pallas import tpu as pltpu

D_MODEL = 1024
DEPTH = 4
GRID_W = 64
ROPE_THETA = 10000.0
CHUNK = 128
LN_EPS = 1e-5

A_HEADS = 4
A_QK_DIM = 32
A_V_DIM = 64
A_WIDTH = A_HEADS * A_V_DIM
B_HEADS = 8
B_KV_HEADS = 2
B_HEAD_DIM = 64
B_GROUP = B_HEADS // B_KV_HEADS
B_WIDTH = B_HEADS * B_HEAD_DIM
B_KV_COLS = B_KV_HEADS * B_HEAD_DIM
C_GROUPS = 4
C_GROUP_DIM = 64
C_WIDTH = C_GROUPS * C_GROUP_DIM
IN_COLS = 2048
FFN_HIDDEN = 2816
DEEPNORM_ALPHA = (2 * DEPTH) ** 0.25

LANES = 128
LOG2E = 1.4426950408889634
VMEM_LIMIT_BYTES = 56 * 1024 * 1024

OFF_AQ, OFF_AK, OFF_AV = 0, 256, 512
OFF_BQ, OFF_BK, OFF_BV = 768, 1280, 1408
OFF_CU, OFF_CV = 1536, 1792

TM_IN = 1024
SUB_IN = 256
TQ = 1024
TQ_A = 512
TM_POST = 1024
SUB_POST = 256

F32 = jnp.float32
BF16 = jnp.bfloat16


def _bq_permutation():
    n = np.arange(B_WIDTH)
    head = n // LANES + B_GROUP * ((n % LANES) // B_HEAD_DIM)
    return head * B_HEAD_DIM + n % B_HEAD_DIM


_BQ_PERM = _bq_permutation()


def _gelu_tanh(x):
    c = math.sqrt(2.0 / math.pi)
    return x * (0.5 * (1.0 + jnp.tanh(c * (x + 0.044715 * (x * x * x)))))


def _layer_norm(x, g, b):
    mu = jnp.mean(x, axis=-1, keepdims=True)
    xc = x - mu
    var = jnp.mean(xc * xc, axis=-1, keepdims=True)
    return xc * lax.rsqrt(var + LN_EPS) * g + b


def _rope128(xb, cos, sin_signed):
    return xb * cos + pltpu.roll(xb, LANES // 2, 1) * sin_signed


def _rms64(xb, lo, g128, post_scale):
    x2 = xb * xb
    s_lo = jnp.sum(jnp.where(lo, x2, 0.0), axis=-1, keepdims=True)
    s_hi = jnp.sum(jnp.where(lo, 0.0, x2), axis=-1, keepdims=True)
    r = lax.rsqrt(jnp.where(lo, s_lo, s_hi) * (1.0 / B_HEAD_DIM) + LN_EPS)
    return xb * r * g128 * post_scale


def _in_kernel(x_ref, w_ref, ta_ref, tb_ref, gq_ref, gk_ref, clg_ref, clb_ref, ws_ref, bsm_ref,
               aq_ref, akt_ref, av_ref, bq_ref, bkt_ref, bv_ref, co_ref, h0_ref, h1_ref,
               *, tm, blocks_per_seq):
    i = pl.program_id(0)
    t0 = (i % blocks_per_seq) * tm
    lane = lax.broadcasted_iota(jnp.int32, (SUB_IN, LANES), 1)
    lo = (lane % B_HEAD_DIM) < B_HEAD_DIM // 2
    group = lax.broadcasted_iota(jnp.int32, (CHUNK, C_WIDTH), 1) // C_GROUP_DIM
    a_scale = A_QK_DIM ** -0.5 * LOG2E
    b_scale = B_HEAD_DIM ** -0.5 * LOG2E
    h_refs = (h0_ref, h1_ref)

    def project(j):
        xb = x_ref[j * SUB_IN:(j + 1) * SUB_IN, :].astype(BF16)
        h_refs[j % 2][...] = jnp.dot(xb, w_ref[...], preferred_element_type=F32)

    def epilogue(j):
        h_ref = h_refs[j % 2]
        r = slice(j * SUB_IN, (j + 1) * SUB_IN)
        rows = pl.ds(pl.multiple_of(t0 + j * SUB_IN, SUB_IN), SUB_IN)

        cos, sin = ta_ref[0, rows, :], ta_ref[1, rows, :]
        for c in range(2):
            sl = slice(c * LANES, (c + 1) * LANES)
            hq = h_ref[:, OFF_AQ + c * LANES:OFF_AQ + (c + 1) * LANES]
            hk = h_ref[:, OFF_AK + c * LANES:OFF_AK + (c + 1) * LANES]
            aq_ref[r, sl] = (_rope128(hq, cos, sin) * a_scale).astype(BF16)
            akt_ref[sl, r] = _rope128(hk, cos, sin).T.astype(BF16)
        av_ref[r, :] = h_ref[:, OFF_AV:OFF_AV + 256].astype(BF16)

        cos, sin = tb_ref[0, rows, :], tb_ref[1, rows, :]
        gq = gq_ref[...]
        for c in range(B_WIDTH // LANES):
            sl = slice(c * LANES, (c + 1) * LANES)
            qn = _rms64(h_ref[:, OFF_BQ + c * LANES:OFF_BQ + (c + 1) * LANES], lo, gq, b_scale)
            bq_ref[r, sl] = _rope128(qn, cos, sin).astype(BF16)
        kn = _rms64(h_ref[:, OFF_BK:OFF_BK + B_KV_COLS], lo, gk_ref[...], 1.0)
        bkt_ref[:, r] = _rope128(kn, cos, sin).T.astype(BF16)
        bv_ref[r, :] = h_ref[:, OFF_BV:OFF_BV + B_KV_COLS].astype(BF16)

        cu = _gelu_tanh(h_ref[:, OFF_CU:OFF_CU + C_WIDTH])
        cv = _layer_norm(_gelu_tanh(h_ref[:, OFF_CV:OFF_CV + C_WIDTH]), clg_ref[...], clb_ref[...]).astype(BF16)
        for ci in range(SUB_IN // CHUNK):
            rs = slice(ci * CHUNK, (ci + 1) * CHUNK)
            vch = cv[rs, :]
            zero = jnp.zeros_like(vch)
            mixed = bsm_ref[...]
            for g in range(C_GROUPS):
                mixed = mixed + jnp.dot(ws_ref[g], jnp.where(group == g, vch, zero),
                                        preferred_element_type=F32)
            co_ref[j * SUB_IN + ci * CHUNK:j * SUB_IN + (ci + 1) * CHUNK, :] = (cu[rs, :] * mixed).astype(BF16)

    n_sub = tm // SUB_IN
    project(0)
    for j in range(n_sub):
        if j + 1 < n_sub:
            project(j + 1)
        epilogue(j)


def _in_call(x2d, w_in, ta, tb, gq, gk, clg, clb, ws, bsm, seq):
    m = x2d.shape[0]
    tm = TM_IN
    const2 = lambda i: (0, 0)
    const3 = lambda i: (0, 0, 0)
    row = lambda i: (i, 0)
    outs = ((256, False), (256, True), (256, False), (B_WIDTH, False), (B_KV_COLS, True),
            (B_KV_COLS, False), (C_WIDTH, False))
    col = lambda i: (0, i)
    return pl.pallas_call(
        functools.partial(_in_kernel, tm=tm, blocks_per_seq=seq // tm),
        grid=(m // tm,),
        in_specs=[
            pl.BlockSpec((tm, D_MODEL), row),
            pl.BlockSpec((D_MODEL, IN_COLS), const2, pipeline_mode=pl.Buffered(1)),
            pl.BlockSpec((2, seq, LANES), const3, pipeline_mode=pl.Buffered(1)),
            pl.BlockSpec((2, seq, LANES), const3, pipeline_mode=pl.Buffered(1)),
            pl.BlockSpec((1, LANES), const2),
            pl.BlockSpec((1, LANES), const2),
            pl.BlockSpec((1, C_WIDTH), const2),
            pl.BlockSpec((1, C_WIDTH), const2),
            pl.BlockSpec((C_GROUPS, CHUNK, CHUNK), const3),
            pl.BlockSpec((CHUNK, C_WIDTH), const2),
        ],
        out_specs=[pl.BlockSpec((w, tm), col) if t else pl.BlockSpec((tm, w), row) for w, t in outs],
        out_shape=[jax.ShapeDtypeStruct((w, m) if t else (m, w), BF16) for w, t in outs],
        scratch_shapes=[pltpu.VMEM((SUB_IN, IN_COLS), F32)] * 2,
        compiler_params=pltpu.CompilerParams(
            dimension_semantics=("arbitrary",), vmem_limit_bytes=VMEM_LIMIT_BYTES),
        name="in_proj",
    )(x2d, w_in, ta, tb, gq, gk, clg, clb, ws, bsm)


def _fill_v_ones(v_ref, v1_ref, col0):
    @pl.when(pl.program_id(1) == 0)
    def _():
        v1_ref[:, :LANES] = v_ref[:, col0:col0 + LANES]
        v1_ref[:, LANES:] = jnp.ones((v_ref.shape[0], LANES), BF16)


def _sweep_units(units, q_ref, kt_ref, s_refs, e_refs):
    def scores(u):
        qcol, krow, mask, _ = units[u]
        qc = q_ref[:, qcol * LANES:(qcol + 1) * LANES]
        lhs = jnp.where(mask, qc, jnp.zeros_like(qc))
        s_refs[u % 2][...] = jnp.dot(lhs, kt_ref[krow * LANES:(krow + 1) * LANES, :],
                                     preferred_element_type=F32)

    def probs(u):
        s = s_refs[u % 2][...]
        e_refs[u % 2][...] = jnp.exp2(s - jnp.max(s, axis=-1, keepdims=True)).astype(BF16)

    def values(u):
        o = jnp.dot(e_refs[u % 2][...], units[u][3][...], preferred_element_type=F32)
        return o[:, :LANES] / o[:, LANES:]

    outs = []
    scores(0)
    for u in range(len(units)):
        if u + 1 < len(units):
            scores(u + 1)
        probs(u)
        outs.append(values(u))
    return outs


SAFE_SHIFT_LOG2 = 40.0


def _attn_a_kernel(lq_ref, g_ref, q_ref, kt_ref, v_ref, o_ref, s0_ref, s1_ref, e0_ref, e1_ref,
                   va_ref, vb_ref, kn_ref, *, lam_init):
    lq = lq_ref[...]
    lam = (jnp.exp(jnp.sum(lq[0:1] * lq[1:2], axis=-1, keepdims=True))
           - jnp.exp(jnp.sum(lq[2:3] * lq[3:4], axis=-1, keepdims=True)) + lam_init)
    tq = q_ref.shape[0]
    half = tq // 2
    lane = lax.broadcasted_iota(jnp.int32, (tq, LANES), 1)
    half_lane = lax.broadcasted_iota(jnp.int32, (half, LANES), 1)
    v1_refs = (va_ref, vb_ref)
    for cb in range(2):
        _fill_v_ones(v_ref, v1_refs[cb], cb * LANES)

    def group_of(ln):
        return (ln % 64) // 16

    def group_mask(h, c, ln):
        return group_of(ln) == 2 * (h % 2) + c

    row_lane = lax.broadcasted_iota(jnp.int32, (1, LANES), 1)

    @pl.when(pl.program_id(1) == 0)
    def _():
        for cb in range(2):
            acc = jnp.zeros((1, LANES), F32)
            for g in range(4):
                parts = [kt_ref[cb * LANES + p * 64 + 16 * g:cb * LANES + p * 64 + 16 * (g + 1), :].astype(F32)
                         for p in range(2)]
                n2 = sum(jnp.sum(p * p, axis=0, keepdims=True) for p in parts)
                acc = jnp.where(group_of(row_lane) == g, jnp.max(n2, axis=-1, keepdims=True), acc)
            kn_ref[cb:cb + 1, :] = acc

    same_group = (group_of(lax.broadcasted_iota(jnp.int32, (LANES, LANES), 0))
                  == group_of(lax.broadcasted_iota(jnp.int32, (LANES, LANES), 1)))
    indicator = jnp.where(same_group, 1.0, 0.0).astype(BF16)
    shifts = []
    for cb in range(2):
        q2 = q_ref[:, cb * LANES:(cb + 1) * LANES].astype(F32)
        n2 = jnp.dot((q2 * q2).astype(BF16), indicator, preferred_element_type=F32)
        shifts.append(jnp.sqrt(jnp.max(n2, axis=0, keepdims=True) * kn_ref[cb:cb + 1, :]) * 1.01)
    worst = jnp.max(jnp.maximum(shifts[0], shifts[1]))

    def finish(heads):
        lo = lane < A_V_DIM
        for cb in range(A_WIDTH // LANES):
            o = jnp.where(lo, heads[2 * cb], heads[2 * cb + 1])
            o2 = o * o
            ms_lo = jnp.sum(jnp.where(lo, o2, 0.0), axis=-1, keepdims=True)
            ms_hi = jnp.sum(jnp.where(lo, 0.0, o2), axis=-1, keepdims=True)
            r = lax.rsqrt(jnp.where(lo, ms_lo, ms_hi) * (1.0 / A_V_DIM) + LN_EPS)
            o_ref[:, cb * LANES:(cb + 1) * LANES] = (o * r * g_ref[...] * (1.0 - lam_init)).astype(BF16)

    def one_pv_per_head():
        steps = [(h, r) for h in range(A_HEADS) for r in range(2)]
        s_refs, w_refs = (s0_ref, s1_ref), (e0_ref, e1_ref)

        def scores(k):
            h, r = steps[k]
            cb = h // 2
            qc = q_ref[r * half:(r + 1) * half, cb * LANES:(cb + 1) * LANES]
            zero = jnp.zeros_like(qc)
            lhs = jnp.concatenate([jnp.where(group_mask(h, c, half_lane), qc, zero) for c in range(2)], axis=0)
            s = jnp.dot(lhs, kt_ref[cb * LANES:(cb + 1) * LANES, :], preferred_element_type=F32)
            for c in range(2):
                lane0 = 16 * (2 * (h % 2) + c)
                s_refs[k % 2][c * half:(c + 1) * half, :] = jnp.exp2(
                    s[c * half:(c + 1) * half] - shifts[cb][:, lane0:lane0 + 1])

        def values(k):
            h, _ = steps[k]
            e = s_refs[k % 2][...]
            l = jnp.sum(e, axis=-1, keepdims=True)
            l1, l2 = l[:half], l[half:]
            w_refs[k % 2][0:half, :] = (e[:half] - (lam * l1 / l2) * e[half:]).astype(BF16)
            o = jnp.dot(w_refs[k % 2][0:half, :], v1_refs[h // 2][:, :LANES], preferred_element_type=F32)
            return o / l1

        parts = []
        scores(0)
        for k in range(len(steps)):
            if k + 1 < len(steps):
                scores(k + 1)
            parts.append(values(k))
        finish([jnp.concatenate(parts[2 * h:2 * h + 2], axis=0) for h in range(A_HEADS)])

    def two_pvs_per_head():
        units = [(h // 2, h // 2, group_mask(h, c, lane), v1_refs[h // 2])
                 for h in range(A_HEADS) for c in range(2)]
        outs = _sweep_units(units, q_ref, kt_ref, (s0_ref, s1_ref), (e0_ref, e1_ref))
        finish([outs[2 * h] - lam * outs[2 * h + 1] for h in range(A_HEADS)])

    lax.cond(worst <= SAFE_SHIFT_LOG2, one_pv_per_head, two_pvs_per_head)


def _attn_scratch(tq, seq):
    return [pltpu.VMEM((tq, seq), F32), pltpu.VMEM((tq, seq), F32),
            pltpu.VMEM((tq, seq), BF16), pltpu.VMEM((tq, seq), BF16)]


def _attn_a_call(lq, g128, aq, akt, av, seq, lam_init):
    m = aq.shape[0]
    nq = seq // TQ_A
    const2 = lambda b, i: (0, 0)
    return pl.pallas_call(
        functools.partial(_attn_a_kernel, lam_init=lam_init),
        grid=(m // seq, nq),
        in_specs=[
            pl.BlockSpec((4, A_QK_DIM), const2),
            pl.BlockSpec((1, LANES), const2),
            pl.BlockSpec((TQ_A, A_WIDTH), lambda b, i: (b * nq + i, 0)),
            pl.BlockSpec((A_WIDTH, seq), lambda b, i: (0, b)),
            pl.BlockSpec((seq, A_WIDTH), lambda b, i: (b, 0)),
        ],
        out_specs=pl.BlockSpec((TQ_A, A_WIDTH), lambda b, i: (b * nq + i, 0)),
        out_shape=jax.ShapeDtypeStruct((m, A_WIDTH), BF16),
        scratch_shapes=_attn_scratch(TQ_A, seq) + [pltpu.VMEM((seq, 2 * LANES), BF16)] * 2
        + [pltpu.VMEM((8, LANES), F32)],
        compiler_params=pltpu.CompilerParams(
            dimension_semantics=("arbitrary", "arbitrary"), vmem_limit_bytes=VMEM_LIMIT_BYTES),
        name="attn_a",
    )(lq, g128, aq, akt, av)


def _attn_b_kernel(q_ref, kt_ref, v_ref, o_ref, s0_ref, s1_ref, e0_ref, e1_ref, v1_ref):
    tq = q_ref.shape[0]
    lane = lax.broadcasted_iota(jnp.int32, (tq, LANES), 1)
    lo = lane < B_HEAD_DIM
    first = (lane % B_HEAD_DIM) < B_HEAD_DIM // 2
    _fill_v_ones(v_ref, v1_ref, 0)
    units = [(c, 0, first if half == 0 else ~first, v1_ref)
             for c in range(B_WIDTH // LANES) for half in range(2)]
    outs = _sweep_units(units, q_ref, kt_ref, (s0_ref, s1_ref), (e0_ref, e1_ref))
    for c in range(B_WIDTH // LANES):
        o_ref[:, c * LANES:(c + 1) * LANES] = jnp.where(lo, outs[2 * c], outs[2 * c + 1]).astype(BF16)


def _attn_b_call(bq, bkt, bv, seq):
    m = bq.shape[0]
    nq = seq // TQ
    return pl.pallas_call(
        _attn_b_kernel,
        grid=(m // seq, nq),
        in_specs=[
            pl.BlockSpec((TQ, B_WIDTH), lambda b, i: (b * nq + i, 0)),
            pl.BlockSpec((B_KV_COLS, seq), lambda b, i: (0, b)),
            pl.BlockSpec((seq, B_KV_COLS), lambda b, i: (b, 0)),
        ],
        out_specs=pl.BlockSpec((TQ, B_WIDTH), lambda b, i: (b * nq + i, 0)),
        out_shape=jax.ShapeDtypeStruct((m, B_WIDTH), BF16),
        scratch_shapes=_attn_scratch(TQ, seq) + [pltpu.VMEM((seq, 2 * LANES), BF16)],
        compiler_params=pltpu.CompilerParams(
            dimension_semantics=("arbitrary", "arbitrary"), vmem_limit_bytes=VMEM_LIMIT_BYTES),
        name="attn_b",
    )(bq, bkt, bv)


def _post_kernel(x_ref, a_ref, b_ref, c_ref, wa_ref, wb_ref, wc_ref, g1_ref, b1_ref,
                 wg_ref, wu_ref, wd_ref, g2_ref, b2_ref, o_ref):
    tm = x_ref.shape[0]
    n_sub = tm // SUB_POST
    rows = [slice(j * SUB_POST, (j + 1) * SUB_POST) for j in range(n_sub)]

    def mix(r):
        return (jnp.dot(a_ref[r, :], wa_ref[...], preferred_element_type=F32)
                + jnp.dot(b_ref[r, :], wb_ref[...], preferred_element_type=F32)
                + jnp.dot(c_ref[r, :], wc_ref[...], preferred_element_type=F32))

    def ln1(r, m):
        return _layer_norm(DEEPNORM_ALPHA * x_ref[r, :] + m, g1_ref[...], b1_ref[...])

    def hidden(y):
        yb = y.astype(BF16)
        gate = jnp.dot(yb, wg_ref[...], preferred_element_type=F32)
        up = jnp.dot(yb, wu_ref[...], preferred_element_type=F32)
        return (gate * jax.nn.sigmoid(gate) * up).astype(BF16)

    def down(hid):
        return jnp.dot(hid, wd_ref[...], preferred_element_type=F32)

    def ln2(r, y, f):
        o_ref[r, :] = _layer_norm(DEEPNORM_ALPHA * y + f, g2_ref[...], b2_ref[...])

    mixes = [mix(r) for r in rows]
    ys = [None] * n_sub
    ys[0] = ln1(rows[0], mixes[0])
    ffn_prev = None
    for j in range(n_sub):
        hid = hidden(ys[j])
        if j + 1 < n_sub:
            ys[j + 1] = ln1(rows[j + 1], mixes[j + 1])
        if j > 0:
            ln2(rows[j - 1], ys[j - 1], ffn_prev)
        ffn_prev = down(hid)
    ln2(rows[-1], ys[-1], ffn_prev)


def _post_call(x2d, ao, bo, co, wa, wb, wc, g1, b1, wg, wu, wd, g2, b2):
    m = x2d.shape[0]
    tm = TM_POST
    row = lambda i: (i, 0)
    const2 = lambda i: (0, 0)

    def resident(shape):
        return pl.BlockSpec(shape, const2, pipeline_mode=pl.Buffered(1))

    return pl.pallas_call(
        _post_kernel,
        grid=(m // tm,),
        in_specs=[
            pl.BlockSpec((tm, D_MODEL), row),
            pl.BlockSpec((tm, A_WIDTH), row),
            pl.BlockSpec((tm, B_WIDTH), row),
            pl.BlockSpec((tm, C_WIDTH), row),
            resident((A_WIDTH, D_MODEL)),
            resident((B_WIDTH, D_MODEL)),
            resident((C_WIDTH, D_MODEL)),
            resident((1, D_MODEL)),
            resident((1, D_MODEL)),
            resident((D_MODEL, FFN_HIDDEN)),
            resident((D_MODEL, FFN_HIDDEN)),
            resident((FFN_HIDDEN, D_MODEL)),
            resident((1, D_MODEL)),
            resident((1, D_MODEL)),
        ],
        out_specs=pl.BlockSpec((tm, D_MODEL), row),
        out_shape=jax.ShapeDtypeStruct((m, D_MODEL), F32),
        compiler_params=pltpu.CompilerParams(
            dimension_semantics=("arbitrary",), vmem_limit_bytes=VMEM_LIMIT_BYTES),
        name="post",
    )(x2d, ao, bo, co, wa, wb, wc, g1, b1, wg, wu, wd, g2, b2)


def _rope_tables(pos, dim):
    inv = 1.0 / (ROPE_THETA ** (jnp.arange(0, dim, 2, dtype=F32) / dim))
    ang = pos.astype(F32)[:, None] * inv[None, :]
    ang = jnp.concatenate([ang, ang], axis=-1)
    return jnp.cos(ang), jnp.sin(ang)


def _lane_dims():
    lane = np.arange(LANES)
    half, g, j = lane // 64, (lane % 64) // 16, lane % 16
    return g, 16 * half + j, half


def _rope_block_tables(cos_groups, sin_groups):
    g, d, half = _lane_dims()
    cos = jnp.stack([cos_groups[gi][:, di] for gi, di in zip(g, d)], axis=1)
    sin = jnp.stack([sin_groups[gi][:, di] for gi, di in zip(g, d)], axis=1)
    return jnp.stack([cos, jnp.where(half == 0, -sin, sin)])


def kernel(x, w_in, w_out, lam_qk, a_subln_g, b_q_norm_g, b_k_norm_g, c_ln_g, c_ln_b, c_w_s, c_b_s,
           ln1_g, ln1_b, w_gate, w_up, w_down, ln2_g, ln2_b):
    bsz, seq, _ = x.shape
    assert seq % TM_IN == 0 and seq % TQ == 0 and seq % GRID_W == 0 and (bsz * seq) % TM_POST == 0
    t = jnp.arange(seq, dtype=jnp.int32)
    cos_a, sin_a = _rope_tables(t, A_QK_DIM)
    cos_r, sin_r = _rope_tables(t // GRID_W, B_HEAD_DIM // 2)
    cos_c, sin_c = _rope_tables(t % GRID_W, B_HEAD_DIM // 2)
    ta = _rope_block_tables([cos_a] * 4, [sin_a] * 4)
    tb = _rope_block_tables([cos_r, cos_c] * 2, [sin_r, sin_c] * 2)

    g, d, _ = _lane_dims()
    block_perm = 32 * g + d
    in_perm = np.arange(IN_COLS)
    for off, width in ((OFF_AQ, 256), (OFF_AK, 256), (OFF_BK, B_KV_COLS)):
        for c in range(width // LANES):
            in_perm[off + c * LANES:off + (c + 1) * LANES] = off + c * LANES + block_perm
    for c in range(B_WIDTH // LANES):
        in_perm[OFF_BQ + c * LANES:OFF_BQ + (c + 1) * LANES] = OFF_BQ + _BQ_PERM[c * LANES + block_perm]
    out_perm = np.arange(D_MODEL)
    out_perm[A_WIDTH:A_WIDTH + B_WIDTH] = A_WIDTH + _BQ_PERM
    gain_perm = block_perm % B_HEAD_DIM
    w_in_b = w_in[:, :, in_perm].astype(BF16)
    w_out_b = w_out[:, out_perm, :].astype(BF16)
    w_gate_b, w_up_b, w_down_b = w_gate.astype(BF16), w_up.astype(BF16), w_down.astype(BF16)
    w_s_b = c_w_s.astype(BF16)

    x2d = x.reshape(bsz * seq, D_MODEL)
    for l in range(DEPTH):
        lam_init = 0.8 - 0.6 * math.exp(-0.3 * l)
        wa = w_out_b[l, :A_WIDTH]
        wb = w_out_b[l, A_WIDTH:A_WIDTH + B_WIDTH]
        wc = w_out_b[l, A_WIDTH + B_WIDTH:]
        bsm = jnp.repeat(c_b_s[l].T, C_GROUP_DIM, axis=1)
        aq, akt, av, bq, bkt, bv, co = _in_call(
            x2d, w_in_b[l], ta, tb,
            b_q_norm_g[l][gain_perm][None], b_k_norm_g[l][gain_perm][None],
            c_ln_g[l][None], c_ln_b[l][None], w_s_b[l], bsm, seq)
        ao = _attn_a_call(lam_qk[l], jnp.tile(a_subln_g[l], 2)[None], aq, akt, av, seq, lam_init)
        bo = _attn_b_call(bq, bkt, bv, seq)
        x2d = _post_call(x2d, ao, bo, co, wa, wb, wc, ln1_g[l][None], ln1_b[l][None],
                         w_gate_b[l], w_up_b[l], w_down_b[l], ln2_g[l][None], ln2_b[l][None])
    return x2d.reshape(bsz, seq, D_MODEL)
```
